```python
import jax, jax.numpy as jnp
from jax import lax
import numpy as np

D_MODEL = 4096
BATCH = 2
SEQ = 8192
DEPTH = 1

HEAD_DIM = 128
N_HEADS_TOTAL = D_MODEL // HEAD_DIM
N_HEADS_DSA = N_HEADS_TOTAL // 2
N_HEADS_SB = N_HEADS_TOTAL - N_HEADS_DSA
KV_LATENT = 512
IDX_HEADS = 32
IDX_DIM = 128
TOPK_MAX = 256
QBLOCK = 128
N_EXPERTS = 32
TOP_K_EXPERTS = 4
D_EXPERT = 1536
SWIGLU_ALPHA = 1.702
SWIGLU_LIMIT = 7.0
MOE_BLOCK = 256
DEEPNORM_ALPHA = (2 * DEPTH) ** 0.25
DEEPNORM_BETA = (8 * DEPTH) ** -0.25
EPS = 1e-5
INDEX_SCALE = (IDX_HEADS * IDX_DIM) ** -0.5

W_QA = N_HEADS_DSA * HEAD_DIM
W_QIDX = IDX_HEADS * IDX_DIM
W_SB = N_HEADS_SB * HEAD_DIM
IN_SIZES = (W_QA, KV_LATENT, W_QIDX, IDX_DIM, IDX_HEADS, W_SB, W_SB, W_SB)
W_IN = sum(IN_SIZES)
SPLIT_POINTS = [int(v) for v in np.cumsum(IN_SIZES)[:-1]]

kernel_name = "hybrid_dsa_stickbreaking_moe_deepnorm_adaln"


def layer_norm(x, g, b):
    xf = x.astype(jnp.float32)
    mu = jnp.mean(xf, axis=-1, keepdims=True)
    var = jnp.mean(jnp.square(xf - mu), axis=-1, keepdims=True)
    return ((xf - mu) * lax.rsqrt(var + EPS) * g + b).astype(x.dtype)


def rms_norm(x, g):
    xf = x.astype(jnp.float32)
    return (xf * lax.rsqrt(jnp.mean(jnp.square(xf), axis=-1, keepdims=True) + EPS) * g).astype(x.dtype)


def alibi_slopes(n):
    return 2.0 ** (-8.0 * jnp.arange(1, n + 1, dtype=jnp.float32) / n)


def clamped_swiglu(u):
    glu, lin = jnp.split(u, 2, axis=-1)
    glu = jnp.minimum(glu, SWIGLU_LIMIT)
    lin = jnp.clip(lin, -SWIGLU_LIMIT, SWIGLU_LIMIT)
    return glu * jax.nn.sigmoid(SWIGLU_ALPHA * glu) * (lin + 1.0)


def dsa_mixer(q_a, ckv, q_idx, k_idx, w_idx, w_uk, w_uv):
    B, S = q_a.shape[0], q_a.shape[1]
    k_sel = min(TOPK_MAX, S // 4)
    slopes = alibi_slopes(N_HEADS_DSA)
    key_pos = jnp.arange(S)

    def block(i):
        t0 = i * QBLOCK
        qa = lax.dynamic_slice_in_dim(q_a, t0, QBLOCK, axis=1)
        qi = lax.dynamic_slice_in_dim(q_idx, t0, QBLOCK, axis=1)
        wi = lax.dynamic_slice_in_dim(w_idx, t0, QBLOCK, axis=1)
        t_pos = t0 + jnp.arange(QBLOCK)
        idx_logits = jnp.einsum('bqhd,bsd->bqhs', qi, k_idx)
        score = jnp.einsum('bqhs,bqh->bqs', jax.nn.relu(idx_logits), wi).astype(jnp.float32) * INDEX_SCALE
        admissible = key_pos[None, :] <= t_pos[:, None]
        score = jnp.where(admissible[None], score, -jnp.inf)
        _, sel = lax.top_k(score, k_sel)
        valid = sel <= t_pos[None, :, None]
        c_sel = jax.vmap(lambda c_b, i_b: c_b[i_b])(ckv, sel)
        q_lat = jnp.einsum('bqhd,hdc->bqhc', qa, w_uk)
        logits = jnp.einsum('bqhc,bqkc->bhqk', q_lat, c_sel).astype(jnp.float32) * HEAD_DIM ** -0.5
        dist = (t_pos[None, :, None] - sel).astype(jnp.float32)
        logits = logits - slopes[None, :, None, None] * dist[:, None]
        logits = jnp.where(valid[:, None], logits, -jnp.inf)
        p = jax.nn.softmax(logits, axis=-1).astype(c_sel.dtype)
        o_lat = jnp.einsum('bhqk,bqkc->bqhc', p, c_sel)
        return jnp.einsum('bqhc,hcd->bqhd', o_lat, w_uv)

    out = lax.map(block, jnp.arange(S // QBLOCK))
    return out.transpose(1, 0, 2, 3, 4).reshape(B, S, W_QA)


def stick_breaking_mixer(q, k, v):
    B, S = q.shape[0], q.shape[1]
    key_pos = jnp.arange(S)

    def block(i):
        t0 = i * QBLOCK
        qb = lax.dynamic_slice_in_dim(q, t0, QBLOCK, axis=1)
        t_pos = t0 + jnp.arange(QBLOCK)
        z = jnp.einsum('bqhd,bshd->bhqs', qb, k).astype(jnp.float32) * HEAD_DIM ** -0.5
        strict = (key_pos[None, :] < t_pos[:, None])[None, None]
        u = jnp.where(strict, jax.nn.softplus(z), 0.0)
        later = lax.cumsum(u, axis=3, reverse=True) - u
        a = jnp.where(strict, jnp.exp(jax.nn.log_sigmoid(z) - later), 0.0).astype(v.dtype)
        return jnp.einsum('bhqs,bshd->bqhd', a, v)

    out = lax.map(block, jnp.arange(S // QBLOCK))
    return out.transpose(1, 0, 2, 3, 4).reshape(B, S, W_SB)


def routed_experts(h, w_router, b_router, w_up, b_up, w_down, b_down):
    B, S, D = h.shape
    n_tok = B * S
    ht = h.reshape(n_tok, D)
    logits = jnp.dot(ht, w_router).astype(jnp.float32) + b_router.astype(jnp.float32)
    top_logit, top_e = lax.top_k(logits, TOP_K_EXPERTS)
    gates = jax.nn.softmax(top_logit, axis=-1).astype(h.dtype)
    n_asg = n_tok * TOP_K_EXPERTS
    e_flat = top_e.reshape(n_asg)
    order = jnp.argsort(e_flat)
    e_sorted = e_flat[order]
    tok_sorted = (order // TOP_K_EXPERTS).astype(jnp.int32)
    gate_sorted = gates.reshape(n_asg)[order]
    counts = jnp.bincount(e_flat, length=N_EXPERTS)
    padded = (counts + MOE_BLOCK - 1) // MOE_BLOCK * MOE_BLOCK
    start = jnp.cumsum(counts) - counts
    pad_end = jnp.cumsum(padded)
    pad_start = pad_end - padded
    dest = pad_start[e_sorted] + jnp.arange(n_asg) - start[e_sorted]
    n_rows = (-(-n_asg // MOE_BLOCK) + N_EXPERTS) * MOE_BLOCK
    n_blocks = n_rows // MOE_BLOCK
    row_tok = jnp.full((n_rows,), n_tok, jnp.int32).at[dest].set(tok_sorted)
    row_gate = jnp.zeros((n_rows,), h.dtype).at[dest].set(gate_sorted)
    blk_e = jnp.minimum(jnp.searchsorted(pad_end, jnp.arange(n_blocks) * MOE_BLOCK, side='right'), N_EXPERTS - 1)
    ht_pad = jnp.concatenate([ht, jnp.zeros((1, D), h.dtype)], axis=0)

    def expert_block(y, blk):
        tok, g, e = blk
        u = ht_pad[tok] @ w_up[e] + b_up[e]
        out = clamped_swiglu(u) @ w_down[e] + b_down[e]
        return y.at[tok].add((out * g[:, None]).astype(y.dtype)), None

    y, _ = lax.scan(expert_block, jnp.zeros_like(ht_pad),
                    (row_tok.reshape(n_blocks, MOE_BLOCK), row_gate.reshape(n_blocks, MOE_BLOCK), blk_e))
    return y[:n_tok].reshape(B, S, D)


def setup_inputs(seed: int = 0) -> dict:
    key = jax.random.key(seed)
    ks = jax.random.split(key, 24)
    n = jax.random.normal
    f32 = jnp.float32
    D, L = D_MODEL, DEPTH
    col_scale = jnp.concatenate([jnp.ones((W_IN - W_SB,), f32), jnp.full((W_SB,), DEEPNORM_BETA, f32)])
    return {
        'x': n(ks[0], (BATCH, SEQ, D), f32),
        'c': n(ks[1], (BATCH, D), f32),
        'w_ada': n(ks[2], (L, D, 6 * D), f32) * (0.5 * D ** -0.5),
        'b_ada': n(ks[3], (L, 6 * D), f32) * 0.02,
        'w_in': n(ks[4], (L, D, W_IN), f32) * D ** -0.5 * col_scale,
        'kv_norm_g': 1.0 + 0.02 * n(ks[5], (L, KV_LATENT), f32),
        'w_uk': n(ks[6], (L, N_HEADS_DSA, HEAD_DIM, KV_LATENT), f32) * KV_LATENT ** -0.5,
        'w_uv': n(ks[7], (L, N_HEADS_DSA, KV_LATENT, HEAD_DIM), f32) * KV_LATENT ** -0.5 * DEEPNORM_BETA,
        'grp_norm_dsa': 1.0 + 0.02 * n(ks[8], (L, W_QA), f32),
        'grp_norm_sb': 1.0 + 0.02 * n(ks[9], (L, W_SB), f32),
        'w_out': n(ks[10], (L, W_QA + W_SB, D), f32) * (W_QA + W_SB) ** -0.5 * DEEPNORM_BETA,
        'ln1_g': 1.0 + 0.02 * n(ks[11], (L, D), f32),
        'ln1_b': 0.02 * n(ks[12], (L, D), f32),
        'w_router': n(ks[13], (L, D, N_EXPERTS), f32) * D ** -0.5,
        'b_router': 0.01 * n(ks[14], (L, N_EXPERTS), f32),
        'w_up': n(ks[15], (L, N_EXPERTS, D, 2 * D_EXPERT), f32) * D ** -0.5,
        'b_up': 0.01 * n(ks[16], (L, N_EXPERTS, 2 * D_EXPERT), f32),
        'w_down': n(ks[17], (L, N_EXPERTS, D_EXPERT, D), f32) * D_EXPERT ** -0.5 * DEEPNORM_BETA,
        'b_down': 0.01 * n(ks[18], (L, N_EXPERTS, D), f32),
        'ln2_g': 1.0 + 0.02 * n(ks[19], (L, D), f32),
        'ln2_b': 0.02 * n(ks[20], (L, D), f32),
    }


def reference(x, c, w_ada, b_ada, w_in, kv_norm_g, w_uk, w_uv, grp_norm_dsa, grp_norm_sb,
              w_out, ln1_g, ln1_b, w_router, b_router, w_up, b_up, w_down, b_down, ln2_g, ln2_b):
    B, S, _ = x.shape
    for l in range(DEPTH):
        mod = jnp.einsum('bd,de->be', jax.nn.silu(c), w_ada[l]) + b_ada[l]
        shift_a, scale_a, gate_a, shift_f, scale_f, gate_f = jnp.split(mod[:, None, :], 6, axis=-1)

        h = x * (1.0 + scale_a) + shift_a
        proj = h @ w_in[l]
        q_a, ckv, q_idx, k_idx, w_idx, q_b, k_b, v_b = jnp.split(proj, SPLIT_POINTS, axis=-1)
        q_a = q_a.reshape(B, S, N_HEADS_DSA, HEAD_DIM)
        ckv = rms_norm(ckv, kv_norm_g[l])
        q_idx = q_idx.reshape(B, S, IDX_HEADS, IDX_DIM)
        q_b = q_b.reshape(B, S, N_HEADS_SB, HEAD_DIM)
        k_b = k_b.reshape(B, S, N_HEADS_SB, HEAD_DIM)
        v_b = v_b.reshape(B, S, N_HEADS_SB, HEAD_DIM)
        o_dsa = dsa_mixer(q_a, ckv, q_idx, k_idx, w_idx, w_uk[l], w_uv[l])
        o_sb = stick_breaking_mixer(q_b, k_b, v_b)
        mixed = jnp.concatenate([rms_norm(o_dsa, grp_norm_dsa[l]), rms_norm(o_sb, grp_norm_sb[l])], axis=-1) @ w_out[l]
        x = layer_norm(DEEPNORM_ALPHA * x + gate_a * mixed, ln1_g[l], ln1_b[l])

        h = x * (1.0 + scale_f) + shift_f
        y = routed_experts(h, w_router[l], b_router[l], w_up[l], b_up[l], w_down[l], b_down[l])
        x = layer_norm(DEEPNORM_ALPHA * x + gate_f * y, ln2_g[l], ln2_b[l])
    return x
```

```python
import functools
import math

import jax
import jax.numpy as jnp
from jax import lax
from jax.experimental import pallas as pl
from jax.experimental.pallas import tpu as pltpu

F32 = jnp.float32
BF16 = jnp.bfloat16

HEAD_DIM = 128
KV_LATENT = 512
IDX_HEADS = 32
IDX_DIM = 128
TOPK_MAX = 256
N_EXPERTS = 32
TOP_K_EXPERTS = 4
SWIGLU_ALPHA = 1.702
SWIGLU_LIMIT = 7.0
DEPTH = 1
DEEPNORM_ALPHA = (2 * DEPTH) ** 0.25
EPS = 1e-5
INDEX_SCALE = (IDX_HEADS * IDX_DIM) ** -0.5
ATTN_SCALE = HEAD_DIM ** -0.5

LANES = 128
VMEM_LIMIT_BYTES = 56 * 1024 * 1024

MASKED_LOGIT = -1e30
F32_EXP_UNDERFLOW = 104.0

NT_DIMS = (((1,), (1,)), ((), ()))


def _cparams(n_axes):
    return pltpu.CompilerParams(
        dimension_semantics=("arbitrary",) * n_axes, vmem_limit_bytes=VMEM_LIMIT_BYTES)


def _split_bf16(v):
    hi = v.astype(BF16)
    lo = (v - hi.astype(F32)).astype(BF16)
    return hi, lo


def _ada_kernel(c_ref, w_ref, b_ref, o_ref):
    c = c_ref[...]
    s = c * (1.0 / (1.0 + jnp.exp(-c)))
    s_hi, s_lo = _split_bf16(s)
    w_hi, w_lo = _split_bf16(w_ref[...])
    acc = jnp.dot(s_hi, w_hi, preferred_element_type=F32)
    acc += jnp.dot(s_hi, w_lo, preferred_element_type=F32)
    acc += jnp.dot(s_lo, w_hi, preferred_element_type=F32)
    o_ref[...] = acc + b_ref[...]


def ada_modulation(c, w_ada, b_ada, tn=512):
    B, D = c.shape
    N = w_ada.shape[1]
    rows = 8
    c_pad = jnp.zeros((rows, D), F32).at[:B].set(c)
    out = pl.pallas_call(
        _ada_kernel,
        grid=(N // tn,),
        in_specs=[pl.BlockSpec((rows, D), lambda j: (0, 0)),
                  pl.BlockSpec((D, tn), lambda j: (0, j)),
                  pl.BlockSpec((1, tn), lambda j: (0, j))],
        out_specs=pl.BlockSpec((rows, tn), lambda j: (0, j)),
        out_shape=jax.ShapeDtypeStruct((rows, N), F32),
        compiler_params=_cparams(1),
        name="ada_modulation",
    )(c_pad, w_ada, b_ada.reshape(1, N))
    return out[:B]


def _modulate_kernel(x_ref, sc_ref, sh_ref, o_ref):
    o_ref[0] = (x_ref[0] * (1.0 + sc_ref[0]) + sh_ref[0]).astype(o_ref.dtype)


def modulate(x, scale, shift, tm=512):
    B, S, D = x.shape
    tm = min(tm, S)
    vec = pl.BlockSpec((1, 1, D), lambda b, i: (b, 0, 0))
    return pl.pallas_call(
        _modulate_kernel,
        grid=(B, S // tm),
        in_specs=[pl.BlockSpec((1, tm, D), lambda b, i: (b, i, 0)), vec, vec],
        out_specs=pl.BlockSpec((1, tm, D), lambda b, i: (b, i, 0)),
        out_shape=jax.ShapeDtypeStruct((B, S, D), BF16),
        compiler_params=_cparams(2),
        name="modulate",
    )(x, scale.reshape(B, 1, D), shift.reshape(B, 1, D))


def _matmul_kernel(a_ref, b_ref, o_ref):
    o_ref[...] = jnp.dot(a_ref[...], b_ref[...], preferred_element_type=F32).astype(o_ref.dtype)


def matmul(a, b, out_dtype, tm=512, tn=512):
    M, K = a.shape
    N = b.shape[1]
    tm, tn = min(tm, M), min(tn, N)
    return pl.pallas_call(
        _matmul_kernel,
        grid=(M // tm, N // tn),
        in_specs=[pl.BlockSpec((tm, K), lambda i, j: (i, 0)),
                  pl.BlockSpec((K, tn), lambda i, j: (0, j))],
        out_specs=pl.BlockSpec((tm, tn), lambda i, j: (i, j)),
        out_shape=jax.ShapeDtypeStruct((M, N), out_dtype),
        compiler_params=_cparams(2),
        name="matmul",
    )(a, b)


def _rmsnorm_kernel(x_ref, g_ref, o_ref):
    x = x_ref[...].astype(F32)
    ms = jnp.mean(x * x, axis=-1, keepdims=True)
    o_ref[...] = (x * lax.rsqrt(ms + EPS) * g_ref[...]).astype(o_ref.dtype)


def rmsnorm(x, g, tm=512):
    M, W = x.shape
    tm = min(tm, M)
    return pl.pallas_call(
        _rmsnorm_kernel,
        grid=(M // tm,),
        in_specs=[pl.BlockSpec((tm, W), lambda i: (i, 0)), pl.BlockSpec((1, W), lambda i: (0, 0))],
        out_specs=pl.BlockSpec((tm, W), lambda i: (i, 0)),
        out_shape=jax.ShapeDtypeStruct((M, W), BF16),
        compiler_params=_cparams(1),
        name="rmsnorm",
    )(x, g.reshape(1, W))


def _resid_ln_kernel(x_ref, r_ref, gate_ref, g_ref, b_ref, *rest, with_mod):
    v = DEEPNORM_ALPHA * x_ref[0] + gate_ref[0] * r_ref[0]
    mu = jnp.mean(v, axis=-1, keepdims=True)
    d = v - mu
    var = jnp.mean(d * d, axis=-1, keepdims=True)
    y = d * lax.rsqrt(var + EPS) * g_ref[...] + b_ref[...]
    if with_mod:
        sc_ref, sh_ref, o_ref, h_ref = rest
        o_ref[0] = y
        h_ref[0] = (y * (1.0 + sc_ref[0]) + sh_ref[0]).astype(h_ref.dtype)
    else:
        (o_ref,) = rest
        o_ref[0] = y


def resid_layernorm(x, r, gate, g, b, scale=None, shift=None, tm=256):
    B, S, D = x.shape
    tm = min(tm, S)
    with_mod = scale is not None
    row = pl.BlockSpec((1, tm, D), lambda bb, i: (bb, i, 0))
    per_batch = pl.BlockSpec((1, 1, D), lambda bb, i: (bb, 0, 0))
    shared = pl.BlockSpec((1, D), lambda bb, i: (0, 0))
    in_specs = [row, row, per_batch, shared, shared]
    args = [x, r, gate.reshape(B, 1, D), g.reshape(1, D), b.reshape(1, D)]
    out_specs = row
    out_shape = jax.ShapeDtypeStruct((B, S, D), F32)
    if with_mod:
        in_specs += [per_batch, per_batch]
        args += [scale.reshape(B, 1, D), shift.reshape(B, 1, D)]
        out_specs = [row, row]
        out_shape = [out_shape, jax.ShapeDtypeStruct((B, S, D), BF16)]
    return pl.pallas_call(
        functools.partial(_resid_ln_kernel, with_mod=with_mod),
        grid=(B, S // tm),
        in_specs=in_specs,
        out_specs=out_specs,
        out_shape=out_shape,
        compiler_params=_cparams(2),
        name="resid_layernorm",
    )(*args)


def _sortable_key(v):
    bits = pltpu.bitcast(v + 0.0, jnp.int32)
    return bits ^ (lax.shift_right_arithmetic(bits, 31) & jnp.int32(0x7FFFFFFF))


def _dsa_kernel(qidx_ref, kidx_ref, widx_ref, qa_ref, ckv_ref, wuk_ref, wuv_ref, g_ref, o_ref,
                key_ref, wb_ref, qlat_ref, p_ref, acc_ref, m_ref, l_ref, obuf_ref,
                *, tq, tk, k_sel, n_heads):
    n_idx = qidx_ref.shape[1]
    lat = ckv_ref.shape[2]
    i = pl.program_id(1)
    t0 = i * tq
    n_chunks = (t0 + tq - 1) // tk + 1
    t_ids = t0 + lax.broadcasted_iota(jnp.int32, (tq, tk), 0)
    lane_ids = lax.broadcasted_iota(jnp.int32, (tq, tk), 1)

    w = widx_ref[0]
    for h in range(n_idx):
        wb_ref[h] = jnp.broadcast_to(w[:, h:h + 1], (tq, LANES))
    q2 = qidx_ref[0].reshape(n_idx * tq, IDX_DIM)

    def score_chunk(j, carry):
        s0 = pl.multiple_of(j * tk, tk)
        kb = kidx_ref[0, pl.ds(s0, tk), :]
        lg = lax.dot_general(q2, kb, NT_DIMS, preferred_element_type=F32)
        r = jnp.maximum(lg, 0.0).reshape(n_idx, tq, tk)
        parts = [jnp.sum(r[:, :, c * LANES:(c + 1) * LANES] * wb_ref[...], axis=0)
                 for c in range(tk // LANES)]
        sc = jnp.concatenate(parts, axis=1) * INDEX_SCALE
        sc = jnp.where(s0 + lane_ids <= t_ids, sc, -jnp.inf)
        key_ref[:, pl.ds(s0, tk)] = _sortable_key(sc)
        return carry

    lax.fori_loop(0, n_chunks, score_chunk, 0)

    sign = jnp.int32(-2 ** 31)

    def bisect(it, prefix):
        cand_u = prefix | lax.shift_left(jnp.int32(1), 31 - it)
        cand = cand_u ^ sign

        def count_chunk(j, cnt):
            s0 = pl.multiple_of(j * tk, tk)
            hit = jnp.where(key_ref[:, pl.ds(s0, tk)] >= cand, 1.0, 0.0)
            for c in range(tk // LANES):
                cnt = cnt + hit[:, c * LANES:(c + 1) * LANES]
            return cnt

        cnt = lax.fori_loop(0, n_chunks, count_chunk, jnp.zeros((tq, LANES), F32))
        total = jnp.sum(cnt, axis=1, keepdims=True)
        return jnp.where(total >= float(k_sel), cand_u, prefix)

    thr = lax.fori_loop(0, 32, bisect, jnp.zeros((tq, 1), jnp.int32)) ^ sign

    for h in range(n_heads):
        ql = jnp.dot(qa_ref[0, h], wuk_ref[h], preferred_element_type=F32)
        qlat_ref[h * tq:(h + 1) * tq, :] = ql.astype(BF16)
    m_ref[...] = jnp.full(m_ref.shape, MASKED_LOGIT, F32)
    l_ref[...] = jnp.zeros(l_ref.shape, F32)
    acc_ref[...] = jnp.zeros(acc_ref.shape, F32)

    def attend_chunk(j, carry):
        s0 = pl.multiple_of(j * tk, tk)
        cb = ckv_ref[0, pl.ds(s0, tk), :]
        lg = lax.dot_general(qlat_ref[...], cb, NT_DIMS, preferred_element_type=F32)
        s_ids = s0 + lane_ids
        sel = (key_ref[:, pl.ds(s0, tk)] >= thr) & (s_ids <= t_ids)
        dist = (t_ids - s_ids).astype(F32)
        alphas = []
        for h in range(n_heads):
            slope = 2.0 ** (-8.0 * (h + 1) / n_heads)
            x = lg[h * tq:(h + 1) * tq, :] * ATTN_SCALE - slope * dist
            x = jnp.where(sel, x, MASKED_LOGIT)
            m_old = m_ref[h]
            m_new = jnp.maximum(m_old, jnp.max(x, axis=1, keepdims=True))
            alpha = jnp.exp(m_old - m_new)
            p = jnp.exp(x - m_new)
            l_ref[h] = alpha * l_ref[h] + jnp.sum(p, axis=1, keepdims=True)
            m_ref[h] = m_new
            p_ref[h * tq:(h + 1) * tq, :] = p.astype(BF16)
            alphas.append(alpha)
        pv = jnp.dot(p_ref[...], cb, preferred_element_type=F32)
        for h in range(n_heads):
            rows = slice(h * tq, (h + 1) * tq)
            acc_ref[rows, :] = acc_ref[rows, :] * alphas[h] + pv[rows, :]
        return carry

    lax.fori_loop(0, n_chunks, attend_chunk, 0)

    ss = jnp.zeros((tq, 1), F32)
    for h in range(n_heads):
        ol = (acc_ref[h * tq:(h + 1) * tq, :] * (1.0 / l_ref[h])).astype(BF16)
        oh = jnp.dot(ol, wuv_ref[h], preferred_element_type=F32)
        obuf_ref[:, h * HEAD_DIM:(h + 1) * HEAD_DIM] = oh
        ss = ss + jnp.sum(oh * oh, axis=1, keepdims=True)
    inv = lax.rsqrt(ss * (1.0 / (n_heads * HEAD_DIM)) + EPS)
    o_ref[0] = (obuf_ref[...] * inv * g_ref[...]).astype(o_ref.dtype)


def dsa_attention(qidx_hm, kidx, widx, qa_hm, ckv, w_uk, w_uv, g, tq=128, tk=256):
    B, n_idx, S, _ = qidx_hm.shape
    n_heads = qa_hm.shape[1]
    lat = ckv.shape[2]
    tq, tk = min(tq, S), min(tk, S)
    k_sel = min(TOPK_MAX, S // 4)
    width = n_heads * HEAD_DIM
    kernel = functools.partial(_dsa_kernel, tq=tq, tk=tk, k_sel=k_sel, n_heads=n_heads)
    return pl.pallas_call(
        kernel,
        grid=(B, S // tq),
        in_specs=[
            pl.BlockSpec((1, n_idx, tq, IDX_DIM), lambda b, i: (b, 0, i, 0)),
            pl.BlockSpec((1, S, IDX_DIM), lambda b, i: (b, 0, 0)),
            pl.BlockSpec((1, tq, n_idx), lambda b, i: (b, i, 0)),
            pl.BlockSpec((1, n_heads, tq, HEAD_DIM), lambda b, i: (b, 0, i, 0)),
            pl.BlockSpec((1, S, lat), lambda b, i: (b, 0, 0)),
            pl.BlockSpec((n_heads, HEAD_DIM, lat), lambda b, i: (0, 0, 0)),
            pl.BlockSpec((n_heads, lat, HEAD_DIM), lambda b, i: (0, 0, 0)),
            pl.BlockSpec((1, width), lambda b, i: (0, 0)),
        ],
        out_specs=pl.BlockSpec((1, tq, width), lambda b, i: (b, i, 0)),
        out_shape=jax.ShapeDtypeStruct((B, S, width), BF16),
        scratch_shapes=[
            pltpu.VMEM((tq, S), jnp.int32),
            pltpu.VMEM((n_idx, tq, LANES), F32),
            pltpu.VMEM((n_heads * tq, lat), BF16),
            pltpu.VMEM((n_heads * tq, tk), BF16),
            pltpu.VMEM((n_heads * tq, lat), F32),
            pltpu.VMEM((n_heads, tq, 1), F32),
            pltpu.VMEM((n_heads, tq, 1), F32),
            pltpu.VMEM((tq, width), F32),
        ],
        compiler_params=_cparams(2),
        name="dsa_attention",
    )(qidx_hm, kidx, widx, qa_hm, ckv, w_uk, w_uv, g.reshape(1, width))


def _sb_kernel(q_ref, k_ref, v_ref, o_ref, acc_ref, carry_ref, *, tq):
    i = pl.program_id(2)
    t0 = i * tq
    q = q_ref[0, 0]
    row = lax.broadcasted_iota(jnp.int32, (tq, tq), 0)
    col = lax.broadcasted_iota(jnp.int32, (tq, tq), 1)
    suffix_ones = jnp.where(row > col, 1.0, 0.0).astype(BF16)
    acc_ref[...] = jnp.zeros(acc_ref.shape, F32)
    carry_ref[...] = jnp.zeros(carry_ref.shape, F32)

    def keep_going(state):
        j, carry_min = state
        return (j >= 0) & (carry_min <= F32_EXP_UNDERFLOW)

    def block(state):
        j, _ = state
        s0 = pl.multiple_of(j * tq, tq)
        kb = k_ref[0, 0, pl.ds(s0, tq), :]
        vb = v_ref[0, 0, pl.ds(s0, tq), :]
        z = lax.dot_general(q, kb, NT_DIMS, preferred_element_type=F32) * ATTN_SCALE
        strict = (s0 + col) < (t0 + row)
        sp = jnp.maximum(z, 0.0) + jnp.log(1.0 + jnp.exp(-jnp.abs(z)))
        u = jnp.where(strict, sp, 0.0)
        u_hi, u_lo = _split_bf16(u)
        within = (jnp.dot(u_hi, suffix_ones, preferred_element_type=F32)
                  + jnp.dot(u_lo, suffix_ones, preferred_element_type=F32))
        carry = carry_ref[...]
        a = jnp.where(strict, jnp.exp(z - sp - (carry + within)), 0.0)
        acc_ref[...] += jnp.dot(a.astype(BF16), vb, preferred_element_type=F32)
        carry = carry + jnp.sum(u, axis=1, keepdims=True)
        carry_ref[...] = carry
        return j - 1, jnp.min(carry)

    lax.while_loop(keep_going, block, (i, jnp.float32(0.0)))
    o_ref[0] = acc_ref[...]


def stick_breaking_attention(q_hm, k_hm, v_hm, tq=256):
    B, H, S, _ = q_hm.shape
    tq = min(tq, S)
    whole = pl.BlockSpec((1, 1, S, HEAD_DIM), lambda b, h, i: (b, h, 0, 0))
    return pl.pallas_call(
        functools.partial(_sb_kernel, tq=tq),
        grid=(B, H, S // tq),
        in_specs=[pl.BlockSpec((1, 1, tq, HEAD_DIM), lambda b, h, i: (b, h, i, 0)), whole, whole],
        out_specs=pl.BlockSpec((1, tq, HEAD_DIM), lambda b, h, i: (b, i, h)),
        out_shape=jax.ShapeDtypeStruct((B, S, H * HEAD_DIM), F32),
        scratch_shapes=[pltpu.VMEM((tq, HEAD_DIM), F32), pltpu.VMEM((tq, 1), F32)],
        compiler_params=_cparams(3),
        name="stick_breaking_attention",
    )(q_hm, k_hm, v_hm)


def _moe_kernel(blk_e_ref, n_used_ref, x_ref, wg_ref, wl_ref, bg_ref, bl_ref, wd_ref, bd_ref,
                gate_ref, o_ref, *, n_chunks):
    r = pl.program_id(0)
    c = pl.program_id(1)
    used = r < n_used_ref[0]

    @pl.when(used)
    def _():
        x = x_ref[...]
        glu = jnp.dot(x, wg_ref[0], preferred_element_type=F32) + bg_ref[0]
        lin = jnp.dot(x, wl_ref[0], preferred_element_type=F32) + bl_ref[0]
        glu = jnp.minimum(glu, SWIGLU_LIMIT)
        lin = jnp.clip(lin, -SWIGLU_LIMIT, SWIGLU_LIMIT)
        act = glu * (1.0 / (1.0 + jnp.exp(-SWIGLU_ALPHA * glu))) * (lin + 1.0)
        part = jnp.dot(act.astype(BF16), wd_ref[0], preferred_element_type=F32)

        @pl.when(c == 0)
        def _():
            o_ref[...] = part + bd_ref[0]

        @pl.when(c > 0)
        def _():
            o_ref[...] += part

        @pl.when(c == n_chunks - 1)
        def _():
            o_ref[...] = o_ref[...] * gate_ref[...]

    @pl.when(jnp.logical_not(used) & (c == 0))
    def _():
        o_ref[...] = jnp.zeros(o_ref.shape, F32)


def moe_experts(xg, blk_e, n_used, row_gate, w_up, b_up, w_down, b_down, tm, tn=256):
    R, D = xg.shape
    d_exp = w_down.shape[1]
    tn = min(tn, d_exp)
    n_chunks = d_exp // tn

    def chunk(r, c, n_used_ref):
        return jnp.where(r < n_used_ref[0], c, n_chunks - 1)

    grid_spec = pltpu.PrefetchScalarGridSpec(
        num_scalar_prefetch=2,
        grid=(R // tm, n_chunks),
        in_specs=[
            pl.BlockSpec((tm, D), lambda r, c, e, n: (r, 0)),
            pl.BlockSpec((1, D, tn), lambda r, c, e, n: (e[r], 0, chunk(r, c, n))),
            pl.BlockSpec((1, D, tn), lambda r, c, e, n: (e[r], 0, n_chunks + chunk(r, c, n))),
            pl.BlockSpec((1, 1, tn), lambda r, c, e, n: (e[r], 0, chunk(r, c, n))),
            pl.BlockSpec((1, 1, tn), lambda r, c, e, n: (e[r], 0, n_chunks + chunk(r, c, n))),
            pl.BlockSpec((1, tn, D), lambda r, c, e, n: (e[r], chunk(r, c, n), 0)),
            pl.BlockSpec((1, 1, D), lambda r, c, e, n: (e[r], 0, 0)),
            pl.BlockSpec((tm, 1), lambda r, c, e, n: (r, 0)),
        ],
        out_specs=pl.BlockSpec((tm, D), lambda r, c, e, n: (r, 0)),
    )
    n_exp = w_up.shape[0]
    return pl.pallas_call(
        functools.partial(_moe_kernel, n_chunks=n_chunks),
        grid_spec=grid_spec,
        out_shape=jax.ShapeDtypeStruct((R, D), F32),
        compiler_params=_cparams(2),
        name="moe_experts",
    )(blk_e, n_used, xg, w_up, w_up, b_up.reshape(n_exp, 1, 2 * d_exp), b_up.reshape(n_exp, 1, 2 * d_exp),
      w_down, b_down.reshape(n_exp, 1, D), row_gate.reshape(R, 1))


def routed_experts(h2, w_router, b_router, w_up, b_up, w_down, b_down, tm=512):
    n_tok, D = h2.shape
    wr = jnp.zeros((D, LANES), BF16).at[:, :N_EXPERTS].set(w_router.astype(BF16))
    logits = matmul(h2, wr, F32, tm=512, tn=LANES)[:, :N_EXPERTS] + b_router.astype(F32)
    top_logit, top_e = lax.top_k(logits, TOP_K_EXPERTS)
    gates = jax.nn.softmax(top_logit, axis=-1)
    n_asg = n_tok * TOP_K_EXPERTS
    e_flat = top_e.reshape(n_asg)
    order = jnp.argsort(e_flat)
    e_sorted = e_flat[order]
    tok_sorted = (order // TOP_K_EXPERTS).astype(jnp.int32)
    gate_sorted = gates.reshape(n_asg)[order]
    counts = jnp.bincount(e_flat, length=N_EXPERTS)
    padded = (counts + tm - 1) // tm * tm
    start = jnp.cumsum(counts) - counts
    pad_end = jnp.cumsum(padded)
    pad_start = pad_end - padded
    dest = (pad_start[e_sorted] + jnp.arange(n_asg) - start[e_sorted]).astype(jnp.int32)
    n_blocks = -(-n_asg // tm) + N_EXPERTS
    n_rows = n_blocks * tm
    row_tok = jnp.full((n_rows,), n_tok, jnp.int32).at[dest].set(tok_sorted)
    row_gate = jnp.zeros((n_rows,), F32).at[dest].set(gate_sorted)
    blk_e = jnp.minimum(jnp.searchsorted(pad_end, jnp.arange(n_blocks) * tm, side='right'),
                        N_EXPERTS - 1).astype(jnp.int32)
    n_used = (pad_end[-1] // tm).astype(jnp.int32).reshape(1)
    pos = jnp.zeros((n_asg,), jnp.int32).at[order].set(dest).reshape(n_tok, TOP_K_EXPERTS)
    h_pad = jnp.concatenate([h2, jnp.zeros((1, D), h2.dtype)], axis=0)
    xg = h_pad[row_tok]
    out_rows = moe_experts(xg, blk_e, n_used, row_gate, w_up, b_up, w_down, b_down, tm)
    return jnp.sum(out_rows[pos], axis=1)


def _heads_major(t, B, S, n_heads):
    return t.reshape(B, S, n_heads, HEAD_DIM).transpose(0, 2, 1, 3)


def _layer(x, c, w_ada, b_ada, w_in, kv_norm_g, w_uk, w_uv, grp_norm_dsa, grp_norm_sb, w_out,
           ln1_g, ln1_b, w_router, b_router, w_up, b_up, w_down, b_down, ln2_g, ln2_b):
    B, S, D = x.shape
    n_tok = B * S
    n_dsa = w_uk.shape[0]
    w_qa = n_dsa * HEAD_DIM
    w_qidx = IDX_HEADS * IDX_DIM
    w_sb = grp_norm_sb.shape[0]
    n_sb = w_sb // HEAD_DIM

    mod = ada_modulation(c, w_ada, b_ada)
    shift_a, scale_a, gate_a, shift_f, scale_f, gate_f = jnp.split(mod, 6, axis=-1)

    o1 = w_qa
    o2 = o1 + KV_LATENT
    o3 = o2 + w_qidx
    o4 = o3 + IDX_DIM
    o5 = o4 + IDX_HEADS
    w_main = jnp.concatenate([w_in[:, :o1], w_in[:, o2:o3], w_in[:, o5:]], axis=1).astype(BF16)
    n_small = KV_LATENT + IDX_DIM + IDX_HEADS
    n_small_pad = -(-n_small // LANES) * LANES
    w_small = jnp.concatenate(
        [w_in[:, o1:o2], w_in[:, o3:o5], jnp.zeros((D, n_small_pad - n_small), F32)], axis=1).astype(BF16)

    h = modulate(x, scale_a, shift_a).reshape(n_tok, D)
    proj = matmul(h, w_main, BF16, tm=512, tn=512)
    small = matmul(h, w_small, F32, tm=512, tn=n_small_pad)

    p0 = w_qa
    p1 = p0 + w_qidx
    p2 = p1 + w_sb
    p3 = p2 + w_sb
    qa_hm = _heads_major(proj[:, :p0], B, S, n_dsa)
    qidx_hm = _heads_major(proj[:, p0:p1], B, S, IDX_HEADS)
    qb_hm = _heads_major(proj[:, p1:p2], B, S, n_sb)
    kb_hm = _heads_major(proj[:, p2:p3], B, S, n_sb)
    vb_hm = _heads_major(proj[:, p3:], B, S, n_sb)
    ckv = rmsnorm(small[:, :KV_LATENT], kv_norm_g).reshape(B, S, KV_LATENT)
    kidx = small[:, KV_LATENT:KV_LATENT + IDX_DIM].astype(BF16).reshape(B, S, IDX_DIM)
    widx = small[:, KV_LATENT + IDX_DIM:n_small].reshape(B, S, IDX_HEADS)

    o_dsa = dsa_attention(qidx_hm, kidx, widx, qa_hm, ckv, w_uk.astype(BF16), w_uv.astype(BF16),
                          grp_norm_dsa)
    o_sb = stick_breaking_attention(qb_hm, kb_hm, vb_hm)
    o_sb = rmsnorm(o_sb.reshape(n_tok, w_sb), grp_norm_sb)
    cat = jnp.concatenate([o_dsa.reshape(n_tok, w_qa), o_sb], axis=-1)
    mixed = matmul(cat, w_out.astype(BF16), F32, tm=512, tn=512).reshape(B, S, D)
    x1, h2 = resid_layernorm(x, mixed, gate_a, ln1_g, ln1_b, scale_f, shift_f)

    y = routed_experts(h2.reshape(n_tok, D), w_router, b_router, w_up.astype(BF16), b_up,
                       w_down.astype(BF16), b_down)
    return resid_layernorm(x1, y.reshape(B, S, D), gate_f, ln2_g, ln2_b)


def kernel(x, c, w_ada, b_ada, w_in, kv_norm_g, w_uk, w_uv, grp_norm_dsa, grp_norm_sb, w_out, ln1_g, ln1_b, w_router, b_router, w_up, b_up, w_down, b_down, ln2_g, ln2_b):
    return _layer(x, c, w_ada[0], b_ada[0], w_in[0], kv_norm_g[0], w_uk[0], w_uv[0],
                  grp_norm_dsa[0], grp_norm_sb[0], w_out[0], ln1_g[0], ln1_b[0], w_router[0],
                  b_router[0], w_up[0], b_up[0], w_down[0], b_down[0], ln2_g[0], ln2_b[0])
```

```python
import functools
import math

import jax
import jax.numpy as jnp
from jax import lax
from jax.experimental import pallas as pl
from jax.experimental.pallas import tpu as pltpu

F32 = jnp.float32
BF16 = jnp.bfloat16

HEAD_DIM = 128
KV_LATENT = 512
IDX_HEADS = 32
IDX_DIM = 128
TOPK_MAX = 256
N_EXPERTS = 32
TOP_K_EXPERTS = 4
SWIGLU_ALPHA = 1.702
SWIGLU_LIMIT = 7.0
DEPTH = 1
DEEPNORM_ALPHA = (2 * DEPTH) ** 0.25
EPS = 1e-5
INDEX_SCALE = (IDX_HEADS * IDX_DIM) ** -0.5
ATTN_SCALE = HEAD_DIM ** -0.5
LOG2_E = math.log2(math.e)

LANES = 128
VMEM_LIMIT_BYTES = 56 * 1024 * 1024

MASKED_LOGIT = -1e30
F32_EXP_UNDERFLOW = 104.0

NT_DIMS = (((1,), (1,)), ((), ()))


def _cparams(n_axes):
    return pltpu.CompilerParams(
        dimension_semantics=("arbitrary",) * n_axes, vmem_limit_bytes=VMEM_LIMIT_BYTES)


def _split_bf16(v):
    hi = v.astype(BF16)
    lo = (v - hi.astype(F32)).astype(BF16)
    return hi, lo


def _ada_kernel(c_ref, w_ref, b_ref, o_ref):
    c = c_ref[...]
    s = c * (1.0 / (1.0 + jnp.exp(-c)))
    s_hi, s_lo = _split_bf16(s)
    w_hi, w_lo = _split_bf16(w_ref[...])
    acc = jnp.dot(s_hi, w_hi, preferred_element_type=F32)
    acc += jnp.dot(s_hi, w_lo, preferred_element_type=F32)
    acc += jnp.dot(s_lo, w_hi, preferred_element_type=F32)
    o_ref[...] = acc + b_ref[...]


def ada_modulation(c, w_ada, b_ada, tn=512):
    B, D = c.shape
    N = w_ada.shape[1]
    rows = 8
    c_pad = jnp.zeros((rows, D), F32).at[:B].set(c)
    out = pl.pallas_call(
        _ada_kernel,
        grid=(N // tn,),
        in_specs=[pl.BlockSpec((rows, D), lambda j: (0, 0)),
                  pl.BlockSpec((D, tn), lambda j: (0, j)),
                  pl.BlockSpec((1, tn), lambda j: (0, j))],
        out_specs=pl.BlockSpec((rows, tn), lambda j: (0, j)),
        out_shape=jax.ShapeDtypeStruct((rows, N), F32),
        compiler_params=_cparams(1),
        name="ada_modulation",
    )(c_pad, w_ada, b_ada.reshape(1, N))
    return out[:B]


def _modulate_kernel(x_ref, sc_ref, sh_ref, o_ref):
    o_ref[0] = (x_ref[0] * (1.0 + sc_ref[0]) + sh_ref[0]).astype(o_ref.dtype)


def modulate(x, scale, shift, tm=512):
    B, S, D = x.shape
    tm = min(tm, S)
    vec = pl.BlockSpec((1, 1, D), lambda b, i: (b, 0, 0))
    return pl.pallas_call(
        _modulate_kernel,
        grid=(B, S // tm),
        in_specs=[pl.BlockSpec((1, tm, D), lambda b, i: (b, i, 0)), vec, vec],
        out_specs=pl.BlockSpec((1, tm, D), lambda b, i: (b, i, 0)),
        out_shape=jax.ShapeDtypeStruct((B, S, D), BF16),
        compiler_params=_cparams(2),
        name="modulate",
    )(x, scale.reshape(B, 1, D), shift.reshape(B, 1, D))


def _matmul_kernel(a_ref, b_ref, o_ref):
    o_ref[...] = jnp.dot(a_ref[...], b_ref[...], preferred_element_type=F32).astype(o_ref.dtype)


def matmul(a, b, out_dtype, tm=512, tn=512):
    M, K = a.shape
    N = b.shape[1]
    tm, tn = min(tm, M), min(tn, N)
    return pl.pallas_call(
        _matmul_kernel,
        grid=(M // tm, N // tn),
        in_specs=[pl.BlockSpec((tm, K), lambda i, j: (i, 0)),
                  pl.BlockSpec((K, tn), lambda i, j: (0, j))],
        out_specs=pl.BlockSpec((tm, tn), lambda i, j: (i, j)),
        out_shape=jax.ShapeDtypeStruct((M, N), out_dtype),
        compiler_params=_cparams(2),
        name="matmul",
    )(a, b)


def _rmsnorm_kernel(x_ref, g_ref, o_ref):
    x = x_ref[...].astype(F32)
    ms = jnp.mean(x * x, axis=-1, keepdims=True)
    o_ref[...] = (x * lax.rsqrt(ms + EPS) * g_ref[...]).astype(o_ref.dtype)


def rmsnorm(x, g, tm=512):
    M, W = x.shape
    tm = min(tm, M)
    return pl.pallas_call(
        _rmsnorm_kernel,
        grid=(M // tm,),
        in_specs=[pl.BlockSpec((tm, W), lambda i: (i, 0)), pl.BlockSpec((1, W), lambda i: (0, 0))],
        out_specs=pl.BlockSpec((tm, W), lambda i: (i, 0)),
        out_shape=jax.ShapeDtypeStruct((M, W), BF16),
        compiler_params=_cparams(1),
        name="rmsnorm",
    )(x, g.reshape(1, W))


def _resid_ln_kernel(x_ref, r_ref, gate_ref, g_ref, b_ref, *rest, with_mod):
    v = DEEPNORM_ALPHA * x_ref[0] + gate_ref[0] * r_ref[0]
    mu = jnp.mean(v, axis=-1, keepdims=True)
    d = v - mu
    var = jnp.mean(d * d, axis=-1, keepdims=True)
    y = d * lax.rsqrt(var + EPS) * g_ref[...] + b_ref[...]
    if with_mod:
        sc_ref, sh_ref, o_ref, h_ref = rest
        o_ref[0] = y
        h_ref[0] = (y * (1.0 + sc_ref[0]) + sh_ref[0]).astype(h_ref.dtype)
    else:
        (o_ref,) = rest
        o_ref[0] = y


def resid_layernorm(x, r, gate, g, b, scale=None, shift=None, tm=256):
    B, S, D = x.shape
    tm = min(tm, S)
    with_mod = scale is not None
    row = pl.BlockSpec((1, tm, D), lambda bb, i: (bb, i, 0))
    per_batch = pl.BlockSpec((1, 1, D), lambda bb, i: (bb, 0, 0))
    shared = pl.BlockSpec((1, D), lambda bb, i: (0, 0))
    in_specs = [row, row, per_batch, shared, shared]
    args = [x, r, gate.reshape(B, 1, D), g.reshape(1, D), b.reshape(1, D)]
    out_specs = row
    out_shape = jax.ShapeDtypeStruct((B, S, D), F32)
    if with_mod:
        in_specs += [per_batch, per_batch]
        args += [scale.reshape(B, 1, D), shift.reshape(B, 1, D)]
        out_specs = [row, row]
        out_shape = [out_shape, jax.ShapeDtypeStruct((B, S, D), BF16)]
    return pl.pallas_call(
        functools.partial(_resid_ln_kernel, with_mod=with_mod),
        grid=(B, S // tm),
        in_specs=in_specs,
        out_specs=out_specs,
        out_shape=out_shape,
        compiler_params=_cparams(2),
        name="resid_layernorm",
    )(*args)


def _sortable_key(v):
    bits = pltpu.bitcast(v + 0.0, jnp.int32)
    return bits ^ (lax.shift_right_arithmetic(bits, 31) & jnp.int32(0x7FFFFFFF))


def _lane_tile(v, width):
    return jnp.concatenate([v] * (width // LANES), axis=1)


def _dsa_kernel(qidx_ref, kidx_ref, widx_ref, qa_ref, ckv_ref, wuk_ref, wuv_ref, g_ref, o_ref,
                key_ref, wb_ref, qlat_ref, lg_ref, bias_ref, p_ref, acc_ref, m_ref, l_ref, alpha_ref,
                slope_ref, obuf_ref, *, tq, tk_score, tk, k_sel, n_heads):
    n_idx = qidx_ref.shape[1]
    i = pl.program_id(1)
    t0 = i * tq
    n_att = (t0 + tq - 1) // tk + 1
    n_score = n_att * (tk // tk_score)

    w = widx_ref[0]
    for h in range(n_idx):
        wb_ref[h] = jnp.broadcast_to(w[:, h:h + 1], (tq, LANES))
    q2 = qidx_ref[0].reshape(n_idx * tq, IDX_DIM)
    t_ids_s = t0 + lax.broadcasted_iota(jnp.int32, (tq, tk_score), 0)
    lane_ids_s = lax.broadcasted_iota(jnp.int32, (tq, tk_score), 1)

    def score_chunk(j, carry):
        s0 = pl.multiple_of(j * tk_score, tk_score)
        kb = kidx_ref[0, pl.ds(s0, tk_score), :]
        lg = lax.dot_general(q2, kb, NT_DIMS, preferred_element_type=F32)
        r = jnp.maximum(lg, 0.0).reshape(n_idx, tq, tk_score)
        parts = [jnp.sum(r[:, :, c * LANES:(c + 1) * LANES] * wb_ref[...], axis=0)
                 for c in range(tk_score // LANES)]
        sc = jnp.concatenate(parts, axis=1) * INDEX_SCALE
        sc = jnp.where(s0 + lane_ids_s <= t_ids_s, sc, -jnp.inf)
        key_ref[:, pl.ds(s0, tk_score)] = _sortable_key(sc)
        return carry

    lax.fori_loop(0, n_score, score_chunk, 0)

    def bisect(it, prefix):
        cand = prefix + lax.shift_left(jnp.int32(1), 31 - it)

        def count_chunk(j, cnt):
            s0 = pl.multiple_of(j * tk, tk)
            keys = key_ref[:, pl.ds(s0, tk)]
            for c in range(tk // LANES):
                cnt = cnt + jnp.where(keys[:, c * LANES:(c + 1) * LANES] >= cand, 1.0, 0.0)
            return cnt

        cnt = lax.fori_loop(0, n_att, count_chunk, jnp.zeros((tq, LANES), F32))
        total = jnp.sum(cnt, axis=1, keepdims=True)
        return jnp.where(total >= float(k_sel), cand, prefix)

    thr = lax.fori_loop(0, 32, bisect, jnp.full((tq, LANES), -2 ** 31, jnp.int32))

    for h in range(n_heads):
        ql = jnp.dot(qa_ref[0, h], wuk_ref[h], preferred_element_type=F32)
        qlat_ref[h * tq:(h + 1) * tq, :] = (ql * (ATTN_SCALE * LOG2_E)).astype(BF16)
    head_no = lax.broadcasted_iota(jnp.int32, slope_ref.shape, 0).astype(F32)
    slope_ref[...] = jnp.exp2(-8.0 * (head_no + 1.0) / n_heads) * LOG2_E
    m_ref[...] = jnp.full(m_ref.shape, MASKED_LOGIT, F32)
    l_ref[...] = jnp.zeros(l_ref.shape, F32)
    acc_ref[...] = jnp.zeros(acc_ref.shape, F32)
    t_ids = t0 + lax.broadcasted_iota(jnp.int32, (tq, tk), 0)
    lane_ids = lax.broadcasted_iota(jnp.int32, (tq, tk), 1)
    col_ids = lax.broadcasted_iota(jnp.int32, (1, tk), 1)

    def attend_chunk(j, carry):
        s0 = pl.multiple_of(j * tk, tk)
        cb = ckv_ref[0, pl.ds(s0, tk), :]
        lg_ref[...] = lax.dot_general(qlat_ref[...], cb, NT_DIMS, preferred_element_type=F32)
        sel = (key_ref[:, pl.ds(s0, tk)] >= _lane_tile(thr, tk)) & (s0 + lane_ids <= t_ids)
        bias_ref[...] = jnp.where(sel, 0.0, MASKED_LOGIT)
        rel = (s0 - t0 + col_ids).astype(F32)

        def head(h, c2):
            rows = pl.ds(pl.multiple_of(h * tq, tq), tq)
            x = lg_ref[rows, :] + _lane_tile(slope_ref[h], tk) * rel + bias_ref[...]
            m_old = m_ref[rows, :]
            m_new = jnp.maximum(m_old, jnp.max(x, axis=1, keepdims=True))
            alpha = jnp.exp2(m_old - m_new)
            p = jnp.exp2(x - _lane_tile(m_new, tk))
            l_ref[rows, :] = alpha * l_ref[rows, :] + jnp.sum(p, axis=1, keepdims=True)
            m_ref[rows, :] = m_new
            alpha_ref[rows, :] = alpha
            p_ref[rows, :] = p.astype(BF16)
            return c2

        lax.fori_loop(0, n_heads, head, 0, unroll=True)
        lat = acc_ref.shape[1]
        acc_ref[...] = (acc_ref[...] * _lane_tile(alpha_ref[...], lat)
                        + jnp.dot(p_ref[...], cb, preferred_element_type=F32))
        return carry

    lax.fori_loop(0, n_att, attend_chunk, 0)

    ss = jnp.zeros((tq, 1), F32)
    for h in range(n_heads):
        rows = slice(h * tq, (h + 1) * tq)
        inv_l = _lane_tile(1.0 / l_ref[rows, :], acc_ref.shape[1])
        oh = jnp.dot((acc_ref[rows, :] * inv_l).astype(BF16), wuv_ref[h],
                     preferred_element_type=F32)
        obuf_ref[:, h * HEAD_DIM:(h + 1) * HEAD_DIM] = oh
        ss = ss + jnp.sum(oh * oh, axis=1, keepdims=True)
    inv = lax.rsqrt(ss * (1.0 / (n_heads * HEAD_DIM)) + EPS)
    o_ref[0] = (obuf_ref[...] * inv * g_ref[...]).astype(o_ref.dtype)


def dsa_attention(qidx_hm, kidx, widx, qa_hm, ckv, w_uk, w_uv, g, tq=128, tk_score=256, tk=512):
    B, n_idx, S, _ = qidx_hm.shape
    n_heads = qa_hm.shape[1]
    lat = ckv.shape[2]
    tq, tk = min(tq, S), min(tk, S)
    tk_score = min(tk_score, tk)
    k_sel = min(TOPK_MAX, S // 4)
    width = n_heads * HEAD_DIM
    kernel = functools.partial(_dsa_kernel, tq=tq, tk_score=tk_score, tk=tk, k_sel=k_sel,
                               n_heads=n_heads)
    once = pl.Buffered(1)
    return pl.pallas_call(
        kernel,
        grid=(B, S // tq),
        in_specs=[
            pl.BlockSpec((1, n_idx, tq, IDX_DIM), lambda b, i: (b, 0, i, 0)),
            pl.BlockSpec((1, S, IDX_DIM), lambda b, i: (b, 0, 0), pipeline_mode=once),
            pl.BlockSpec((1, tq, n_idx), lambda b, i: (b, i, 0)),
            pl.BlockSpec((1, n_heads, tq, HEAD_DIM), lambda b, i: (b, 0, i, 0)),
            pl.BlockSpec((1, S, lat), lambda b, i: (b, 0, 0), pipeline_mode=once),
            pl.BlockSpec((n_heads, HEAD_DIM, lat), lambda b, i: (0, 0, 0), pipeline_mode=once),
            pl.BlockSpec((n_heads, lat, HEAD_DIM), lambda b, i: (0, 0, 0), pipeline_mode=once),
            pl.BlockSpec((1, width), lambda b, i: (0, 0)),
        ],
        out_specs=pl.BlockSpec((1, tq, width), lambda b, i: (b, i, 0)),
        out_shape=jax.ShapeDtypeStruct((B, S, width), BF16),
        scratch_shapes=[
            pltpu.VMEM((tq, S), jnp.int32),
            pltpu.VMEM((n_idx, tq, LANES), F32),
            pltpu.VMEM((n_heads * tq, lat), BF16),
            pltpu.VMEM((n_heads * tq, tk), F32),
            pltpu.VMEM((tq, tk), F32),
            pltpu.VMEM((n_heads * tq, tk), BF16),
            pltpu.VMEM((n_heads * tq, lat), F32),
            pltpu.VMEM((n_heads * tq, LANES), F32),
            pltpu.VMEM((n_heads * tq, LANES), F32),
            pltpu.VMEM((n_heads * tq, LANES), F32),
            pltpu.VMEM((n_heads, 1, LANES), F32),
            pltpu.VMEM((tq, width), F32),
        ],
        compiler_params=_cparams(2),
        name="dsa_attention",
    )(qidx_hm, kidx, widx, qa_hm, ckv, w_uk, w_uv, g.reshape(1, width))


def _sb_kernel(q_ref, k_ref, v_ref, o_ref, acc_ref, carry_ref, *, tq):
    i = pl.program_id(2)
    t0 = i * tq
    q = q_ref[0, 0]
    row = lax.broadcasted_iota(jnp.int32, (tq, tq), 0)
    col = lax.broadcasted_iota(jnp.int32, (tq, tq), 1)
    suffix_ones = jnp.where(row > col, 1.0, 0.0).astype(BF16)
    acc_ref[...] = jnp.zeros(acc_ref.shape, F32)
    carry_ref[...] = jnp.zeros(carry_ref.shape, F32)

    def keep_going(state):
        j, carry_min = state
        return (j >= 0) & (carry_min <= F32_EXP_UNDERFLOW)

    def block(state):
        j, _ = state
        s0 = pl.multiple_of(j * tq, tq)
        kb = k_ref[0, 0, pl.ds(s0, tq), :]
        vb = v_ref[0, 0, pl.ds(s0, tq), :]
        z = lax.dot_general(q, kb, NT_DIMS, preferred_element_type=F32) * ATTN_SCALE
        strict = (s0 + col) < (t0 + row)
        sp = jnp.maximum(z, 0.0) + jnp.log(1.0 + jnp.exp(-jnp.abs(z)))
        u = jnp.where(strict, sp, 0.0)
        u_hi, u_lo = _split_bf16(u)
        within = (jnp.dot(u_hi, suffix_ones, preferred_element_type=F32)
                  + jnp.dot(u_lo, suffix_ones, preferred_element_type=F32))
        carry = carry_ref[...]
        a = jnp.where(strict, jnp.exp(z - sp - (carry + within)), 0.0)
        acc_ref[...] += jnp.dot(a.astype(BF16), vb, preferred_element_type=F32)
        carry = carry + jnp.sum(u, axis=1, keepdims=True)
        carry_ref[...] = carry
        return j - 1, jnp.min(carry)

    lax.while_loop(keep_going, block, (i, jnp.float32(0.0)))
    o_ref[0] = acc_ref[...]


def stick_breaking_attention(q_hm, k_hm, v_hm, tq=256):
    B, H, S, _ = q_hm.shape
    tq = min(tq, S)
    whole = pl.BlockSpec((1, 1, S, HEAD_DIM), lambda b, h, i: (b, h, 0, 0))
    return pl.pallas_call(
        functools.partial(_sb_kernel, tq=tq),
        grid=(B, H, S // tq),
        in_specs=[pl.BlockSpec((1, 1, tq, HEAD_DIM), lambda b, h, i: (b, h, i, 0)), whole, whole],
        out_specs=pl.BlockSpec((1, tq, HEAD_DIM), lambda b, h, i: (b, i, h)),
        out_shape=jax.ShapeDtypeStruct((B, S, H * HEAD_DIM), F32),
        scratch_shapes=[pltpu.VMEM((tq, HEAD_DIM), F32), pltpu.VMEM((tq, 1), F32)],
        compiler_params=_cparams(3),
        name="stick_breaking_attention",
    )(q_hm, k_hm, v_hm)


def _expert_changed(blk_e_ref, r):
    return (r == 0) | (blk_e_ref[r] != blk_e_ref[jnp.maximum(r - 1, 0)])


def _moe_up_kernel(blk_e_ref, n_used_ref, x_ref, wg_ref, wl_ref, bg_ref, bl_ref, o_ref, wg_bf, wl_bf):
    r = pl.program_id(1)
    used = r < n_used_ref[0]

    @pl.when(used & _expert_changed(blk_e_ref, r))
    def _():
        wg_bf[...] = wg_ref[0].astype(BF16)
        wl_bf[...] = wl_ref[0].astype(BF16)

    @pl.when(used)
    def _():
        x = x_ref[...]
        glu = jnp.dot(x, wg_bf[...], preferred_element_type=F32) + bg_ref[0]
        lin = jnp.dot(x, wl_bf[...], preferred_element_type=F32) + bl_ref[0]
        glu = jnp.minimum(glu, SWIGLU_LIMIT)
        lin = jnp.clip(lin, -SWIGLU_LIMIT, SWIGLU_LIMIT)
        act = glu * (1.0 / (1.0 + jnp.exp(-SWIGLU_ALPHA * glu))) * (lin + 1.0)
        o_ref[...] = act.astype(o_ref.dtype)

    @pl.when(jnp.logical_not(used))
    def _():
        o_ref[...] = jnp.zeros(o_ref.shape, o_ref.dtype)


def _moe_down_kernel(blk_e_ref, n_used_ref, a_ref, wd_ref, bd_ref, gate_ref, o_ref, wd_bf):
    r = pl.program_id(1)
    used = r < n_used_ref[0]

    @pl.when(used & _expert_changed(blk_e_ref, r))
    def _():
        wd_bf[...] = wd_ref[0].astype(BF16)

    @pl.when(used)
    def _():
        out = jnp.dot(a_ref[...], wd_bf[...], preferred_element_type=F32) + bd_ref[0]
        o_ref[...] = out * gate_ref[...]

    @pl.when(jnp.logical_not(used))
    def _():
        o_ref[...] = jnp.zeros(o_ref.shape, o_ref.dtype)


def moe_experts(xg, blk_e, n_used, row_gate, w_up, b_up, w_down, b_down, tm, tn_up=512, tn_down=1024):
    R, D = xg.shape
    n_exp, d_exp = w_down.shape[0], w_down.shape[1]
    tn_up, tn_down = min(tn_up, d_exp), min(tn_down, D)
    n_up = d_exp // tn_up

    def row(r, n):
        return jnp.minimum(r, n[0] - 1)

    def expert(r, e, n):
        return e[row(r, n)]

    b_up3 = b_up.reshape(n_exp, 1, 2 * d_exp)
    act = pl.pallas_call(
        _moe_up_kernel,
        grid_spec=pltpu.PrefetchScalarGridSpec(
            num_scalar_prefetch=2,
            grid=(n_up, R // tm),
            in_specs=[
                pl.BlockSpec((tm, D), lambda c, r, e, n: (row(r, n), 0)),
                pl.BlockSpec((1, D, tn_up), lambda c, r, e, n: (expert(r, e, n), 0, c)),
                pl.BlockSpec((1, D, tn_up), lambda c, r, e, n: (expert(r, e, n), 0, n_up + c)),
                pl.BlockSpec((1, 1, tn_up), lambda c, r, e, n: (expert(r, e, n), 0, c)),
                pl.BlockSpec((1, 1, tn_up), lambda c, r, e, n: (expert(r, e, n), 0, n_up + c)),
            ],
            out_specs=pl.BlockSpec((tm, tn_up), lambda c, r, e, n: (r, c)),
            scratch_shapes=[pltpu.VMEM((D, tn_up), BF16), pltpu.VMEM((D, tn_up), BF16)],
        ),
        out_shape=jax.ShapeDtypeStruct((R, d_exp), BF16),
        compiler_params=_cparams(2),
        name="moe_up",
    )(blk_e, n_used, xg, w_up, w_up, b_up3, b_up3)
    return pl.pallas_call(
        _moe_down_kernel,
        grid_spec=pltpu.PrefetchScalarGridSpec(
            num_scalar_prefetch=2,
            grid=(D // tn_down, R // tm),
            in_specs=[
                pl.BlockSpec((tm, d_exp), lambda c, r, e, n: (row(r, n), 0)),
                pl.BlockSpec((1, d_exp, tn_down), lambda c, r, e, n: (expert(r, e, n), 0, c)),
                pl.BlockSpec((1, 1, tn_down), lambda c, r, e, n: (expert(r, e, n), 0, c)),
                pl.BlockSpec((tm, 1), lambda c, r, e, n: (row(r, n), 0)),
            ],
            out_specs=pl.BlockSpec((tm, tn_down), lambda c, r, e, n: (r, c)),
            scratch_shapes=[pltpu.VMEM((d_exp, tn_down), BF16)],
        ),
        out_shape=jax.ShapeDtypeStruct((R, D), F32),
        compiler_params=_cparams(2),
        name="moe_down",
    )(blk_e, n_used, act, w_down, b_down.reshape(n_exp, 1, D), row_gate.reshape(R, 1))


def routed_experts(h2, w_router, b_router, w_up, b_up, w_down, b_down, tm=512):
    n_tok, D = h2.shape
    wr = jnp.zeros((D, LANES), BF16).at[:, :N_EXPERTS].set(w_router.astype(BF16))
    logits = matmul(h2, wr, F32, tm=512, tn=LANES)[:, :N_EXPERTS] + b_router.astype(F32)
    top_logit, top_e = lax.top_k(logits, TOP_K_EXPERTS)
    gates = jax.nn.softmax(top_logit, axis=-1)
    n_asg = n_tok * TOP_K_EXPERTS
    e_flat = top_e.reshape(n_asg)
    order = jnp.argsort(e_flat)
    e_sorted = e_flat[order]
    tok_sorted = (order // TOP_K_EXPERTS).astype(jnp.int32)
    gate_sorted = gates.reshape(n_asg)[order]
    counts = jnp.bincount(e_flat, length=N_EXPERTS)
    padded = (counts + tm - 1) // tm * tm
    start = jnp.cumsum(counts) - counts
    pad_end = jnp.cumsum(padded)
    pad_start = pad_end - padded
    dest = (pad_start[e_sorted] + jnp.arange(n_asg) - start[e_sorted]).astype(jnp.int32)
    n_blocks = -(-n_asg // tm) + N_EXPERTS
    n_rows = n_blocks * tm
    row_tok = jnp.full((n_rows,), n_tok, jnp.int32).at[dest].set(tok_sorted)
    row_gate = jnp.zeros((n_rows,), F32).at[dest].set(gate_sorted)
    blk_e = jnp.minimum(jnp.searchsorted(pad_end, jnp.arange(n_blocks) * tm, side='right'),
                        N_EXPERTS - 1).astype(jnp.int32)
    n_used = (pad_end[-1] // tm).astype(jnp.int32).reshape(1)
    pos = jnp.zeros((n_asg,), jnp.int32).at[order].set(dest).reshape(n_tok, TOP_K_EXPERTS)
    h_pad = jnp.concatenate([h2, jnp.zeros((1, D), h2.dtype)], axis=0)
    xg = h_pad[row_tok]
    out_rows = moe_experts(xg, blk_e, n_used, row_gate, w_up, b_up, w_down, b_down, tm)
    return jnp.sum(out_rows[pos], axis=1)


def _heads_major(t, B, S, n_heads):
    return t.reshape(B, S, n_heads, HEAD_DIM).transpose(0, 2, 1, 3)


def _layer(x, c, w_ada, b_ada, w_in, kv_norm_g, w_uk, w_uv, grp_norm_dsa, grp_norm_sb, w_out,
           ln1_g, ln1_b, w_router, b_router, w_up, b_up, w_down, b_down, ln2_g, ln2_b):
    B, S, D = x.shape
    n_tok = B * S
    n_dsa = w_uk.shape[0]
    w_qa = n_dsa * HEAD_DIM
    w_qidx = IDX_HEADS * IDX_DIM
    w_sb = grp_norm_sb.shape[0]
    n_sb = w_sb // HEAD_DIM

    mod = ada_modulation(c, w_ada, b_ada)
    shift_a, scale_a, gate_a, shift_f, scale_f, gate_f = jnp.split(mod, 6, axis=-1)

    o1 = w_qa
    o2 = o1 + KV_LATENT
    o3 = o2 + w_qidx
    o4 = o3 + IDX_DIM
    o5 = o4 + IDX_HEADS
    w_main = jnp.concatenate([w_in[:, :o1], w_in[:, o2:o3], w_in[:, o5:]], axis=1).astype(BF16)
    n_small = KV_LATENT + IDX_DIM + IDX_HEADS
    n_small_pad = -(-n_small // LANES) * LANES
    w_small = jnp.concatenate(
        [w_in[:, o1:o2], w_in[:, o3:o5], jnp.zeros((D, n_small_pad - n_small), F32)], axis=1).astype(BF16)

    h = modulate(x, scale_a, shift_a).reshape(n_tok, D)
    proj = matmul(h, w_main, BF16, tm=512, tn=512)
    small = matmul(h, w_small, F32, tm=512, tn=n_small_pad)

    p0 = w_qa
    p1 = p0 + w_qidx
    p2 = p1 + w_sb
    p3 = p2 + w_sb
    qa_hm = _heads_major(proj[:, :p0], B, S, n_dsa)
    qidx_hm = _heads_major(proj[:, p0:p1], B, S, IDX_HEADS)
    qb_hm = _heads_major(proj[:, p1:p2], B, S, n_sb)
    kb_hm = _heads_major(proj[:, p2:p3], B, S, n_sb)
    vb_hm = _heads_major(proj[:, p3:], B, S, n_sb)
    ckv = rmsnorm(small[:, :KV_LATENT], kv_norm_g).reshape(B, S, KV_LATENT)
    kidx = small[:, KV_LATENT:KV_LATENT + IDX_DIM].astype(BF16).reshape(B, S, IDX_DIM)
    widx = small[:, KV_LATENT + IDX_DIM:n_small].reshape(B, S, IDX_HEADS)

    o_dsa = dsa_attention(qidx_hm, kidx, widx, qa_hm, ckv, w_uk.astype(BF16), w_uv.astype(BF16),
                          grp_norm_dsa)
    o_sb = stick_breaking_attention(qb_hm, kb_hm, vb_hm)
    o_sb = rmsnorm(o_sb.reshape(n_tok, w_sb), grp_norm_sb)
    cat = jnp.concatenate([o_dsa.reshape(n_tok, w_qa), o_sb], axis=-1)
    mixed = matmul(cat, w_out.astype(BF16), F32, tm=512, tn=512).reshape(B, S, D)
    x1, h2 = resid_layernorm(x, mixed, gate_a, ln1_g, ln1_b, scale_f, shift_f)

    y = routed_experts(h2.reshape(n_tok, D), w_router, b_router, w_up, b_up, w_down, b_down)
    return resid_layernorm(x1, y.reshape(B, S, D), gate_f, ln2_g, ln2_b)


def kernel(x, c, w_ada, b_ada, w_in, kv_norm_g, w_uk, w_uv, grp_norm_dsa, grp_norm_sb, w_out, ln1_g, ln1_b, w_router, b_router, w_up, b_up, w_down, b_down, ln2_g, ln2_b):
    return _layer(x, c, w_ada[0], b_ada[0], w_in[0], kv_norm_g[0], w_uk[0], w_uv[0],
                  grp_norm_dsa[0], grp_norm_sb[0], w_out[0], ln1_g[0], ln1_b[0], w_router[0],
                  b_router[0], w_up[0], b_up[0], w_down[0], b_down[0], ln2_g[0], ln2_b[0])
```

```python
import functools
import math

import jax
import jax.numpy as jnp
from jax import lax
from jax.experimental import pallas as pl
from jax.experimental.pallas import tpu as pltpu

F32 = jnp.float32
BF16 = jnp.bfloat16

HEAD_DIM = 128
KV_LATENT = 512
IDX_HEADS = 32
IDX_DIM = 128
TOPK_MAX = 256
N_EXPERTS = 32
TOP_K_EXPERTS = 4
SWIGLU_ALPHA = 1.702
SWIGLU_LIMIT = 7.0
DEPTH = 1
DEEPNORM_ALPHA = (2 * DEPTH) ** 0.25
EPS = 1e-5
INDEX_SCALE = (IDX_HEADS * IDX_DIM) ** -0.5
ATTN_SCALE = HEAD_DIM ** -0.5
LOG2_E = math.log2(math.e)

LANES = 128
VMEM_LIMIT_BYTES = 56 * 1024 * 1024

MASKED_LOGIT = -1e30
F32_EXP_UNDERFLOW = 104.0

NT_DIMS = (((1,), (1,)), ((), ()))


def _cparams(n_axes):
    return pltpu.CompilerParams(
        dimension_semantics=("arbitrary",) * n_axes, vmem_limit_bytes=VMEM_LIMIT_BYTES)


def _split_bf16(v):
    hi = v.astype(BF16)
    lo = (v - hi.astype(F32)).astype(BF16)
    return hi, lo


def _lane_tile(v, width):
    return jnp.concatenate([v] * (width // LANES), axis=1)


def _ada_kernel(c_ref, w_ref, b_ref, o_ref):
    c = c_ref[...]
    s = c * (1.0 / (1.0 + jnp.exp(-c)))
    s_hi, s_lo = _split_bf16(s)
    w_hi, w_lo = _split_bf16(w_ref[...])
    acc = jnp.dot(s_hi, w_hi, preferred_element_type=F32)
    acc += jnp.dot(s_hi, w_lo, preferred_element_type=F32)
    acc += jnp.dot(s_lo, w_hi, preferred_element_type=F32)
    o_ref[...] = acc + b_ref[...]


def ada_modulation(c, w_ada, b_ada, tn=512):
    B, D = c.shape
    N = w_ada.shape[1]
    rows = 8
    c_pad = jnp.zeros((rows, D), F32).at[:B].set(c)
    out = pl.pallas_call(
        _ada_kernel,
        grid=(N // tn,),
        in_specs=[pl.BlockSpec((rows, D), lambda j: (0, 0)),
                  pl.BlockSpec((D, tn), lambda j: (0, j)),
                  pl.BlockSpec((1, tn), lambda j: (0, j))],
        out_specs=pl.BlockSpec((rows, tn), lambda j: (0, j)),
        out_shape=jax.ShapeDtypeStruct((rows, N), F32),
        compiler_params=_cparams(1),
        name="ada_modulation",
    )(c_pad, w_ada, b_ada.reshape(1, N))
    return out[:B]


def _modulate_kernel(x_ref, sc_ref, sh_ref, o_ref):
    o_ref[0] = (x_ref[0] * (1.0 + sc_ref[0]) + sh_ref[0]).astype(o_ref.dtype)


def modulate(x, scale, shift, tm=512):
    B, S, D = x.shape
    tm = min(tm, S)
    vec = pl.BlockSpec((1, 1, D), lambda b, i: (b, 0, 0))
    return pl.pallas_call(
        _modulate_kernel,
        grid=(B, S // tm),
        in_specs=[pl.BlockSpec((1, tm, D), lambda b, i: (b, i, 0)), vec, vec],
        out_specs=pl.BlockSpec((1, tm, D), lambda b, i: (b, i, 0)),
        out_shape=jax.ShapeDtypeStruct((B, S, D), BF16),
        compiler_params=_cparams(2),
        name="modulate",
    )(x, scale.reshape(B, 1, D), shift.reshape(B, 1, D))


def _matmul_kernel(a_ref, b_ref, o_ref):
    o_ref[...] = jnp.dot(a_ref[...], b_ref[...], preferred_element_type=F32).astype(o_ref.dtype)


def matmul(a, b, out_dtype, tm=512, tn=512):
    M, K = a.shape
    N = b.shape[1]
    tm, tn = min(tm, M), min(tn, N)
    return pl.pallas_call(
        _matmul_kernel,
        grid=(M // tm, N // tn),
        in_specs=[pl.BlockSpec((tm, K), lambda i, j: (i, 0)),
                  pl.BlockSpec((K, tn), lambda i, j: (0, j))],
        out_specs=pl.BlockSpec((tm, tn), lambda i, j: (i, j)),
        out_shape=jax.ShapeDtypeStruct((M, N), out_dtype),
        compiler_params=_cparams(2),
        name="matmul",
    )(a, b)


def _kv_prep_kernel(s_ref, g_ref, ckv_ref, kidx_ref):
    x = s_ref[:, :KV_LATENT]
    ms = jnp.mean(x * x, axis=-1, keepdims=True)
    ckv_ref[...] = (x * lax.rsqrt(ms + EPS) * g_ref[...]).astype(BF16)
    kidx_ref[...] = s_ref[:, KV_LATENT:KV_LATENT + IDX_DIM].astype(BF16)


def kv_prep(small, g, tm=512):
    M, W = small.shape
    tm = min(tm, M)
    return pl.pallas_call(
        _kv_prep_kernel,
        grid=(M // tm,),
        in_specs=[pl.BlockSpec((tm, W), lambda i: (i, 0)), pl.BlockSpec((1, KV_LATENT), lambda i: (0, 0))],
        out_specs=[pl.BlockSpec((tm, KV_LATENT), lambda i: (i, 0)), pl.BlockSpec((tm, IDX_DIM), lambda i: (i, 0))],
        out_shape=[jax.ShapeDtypeStruct((M, KV_LATENT), BF16), jax.ShapeDtypeStruct((M, IDX_DIM), BF16)],
        compiler_params=_cparams(1),
        name="kv_prep",
    )(small, g.reshape(1, KV_LATENT))


def _sortable_key(v):
    bits = pltpu.bitcast(v + 0.0, jnp.int32)
    return bits ^ (lax.shift_right_arithmetic(bits, 31) & jnp.int32(0x7FFFFFFF))


def _dsa_kernel(qidx_ref, kidx_ref, widx_ref, qa_ref, ckv_ref, wuk_ref, wuv_ref, g_ref, o_ref,
                key_ref, wb_ref, q2_ref, qlat_ref, lg_ref, bias_ref, p_ref, acc_ref, m_ref, l_ref,
                alpha_ref, slope_ref, obuf_ref, *, tq, tk_score, tk, k_sel, n_heads, n_idx):
    i = pl.program_id(1)
    t0 = i * tq
    n_att = (t0 + tq - 1) // tk + 1
    n_score = n_att * (tk // tk_score)

    w = widx_ref[0]
    for h in range(n_idx):
        wb_ref[h] = jnp.broadcast_to(w[:, h:h + 1], (tq, LANES))
        q2_ref[h * tq:(h + 1) * tq, :] = qidx_ref[0, :, h * IDX_DIM:(h + 1) * IDX_DIM]
    t_ids_s = t0 + lax.broadcasted_iota(jnp.int32, (tq, tk_score), 0)
    lane_ids_s = lax.broadcasted_iota(jnp.int32, (tq, tk_score), 1)

    def score_chunk(j, carry):
        s0 = pl.multiple_of(j * tk_score, tk_score)
        kb = kidx_ref[0, pl.ds(s0, tk_score), :]
        lg = lax.dot_general(q2_ref[...], kb, NT_DIMS, preferred_element_type=F32)
        r = jnp.maximum(lg, 0.0).reshape(n_idx, tq, tk_score)
        parts = [jnp.sum(r[:, :, c * LANES:(c + 1) * LANES] * wb_ref[...], axis=0)
                 for c in range(tk_score // LANES)]
        sc = jnp.concatenate(parts, axis=1) * INDEX_SCALE
        sc = jnp.where(s0 + lane_ids_s <= t_ids_s, sc, -jnp.inf)
        key_ref[:, pl.ds(s0, tk_score)] = _sortable_key(sc)
        return carry

    lax.fori_loop(0, n_score, score_chunk, 0)

    def bisect(it, prefix):
        cand = prefix + lax.shift_left(jnp.int32(1), 31 - it)

        def count_chunk(j, cnt):
            s0 = pl.multiple_of(j * tk, tk)
            keys = key_ref[:, pl.ds(s0, tk)]
            for c in range(tk // LANES):
                cnt = cnt + jnp.where(keys[:, c * LANES:(c + 1) * LANES] >= cand, 1.0, 0.0)
            return cnt

        cnt = lax.fori_loop(0, n_att, count_chunk, jnp.zeros((tq, LANES), F32))
        total = jnp.sum(cnt, axis=1, keepdims=True)
        return jnp.where(total >= float(k_sel), cand, prefix)

    thr = lax.fori_loop(0, 32, bisect, jnp.full((tq, LANES), -2 ** 31, jnp.int32))

    for h in range(n_heads):
        ql = jnp.dot(qa_ref[0, :, h * HEAD_DIM:(h + 1) * HEAD_DIM], wuk_ref[h], preferred_element_type=F32)
        qlat_ref[h * tq:(h + 1) * tq, :] = (ql * (ATTN_SCALE * LOG2_E)).astype(BF16)
    head_no = lax.broadcasted_iota(jnp.int32, slope_ref.shape, 0).astype(F32)
    slope_ref[...] = jnp.exp2(-8.0 * (head_no + 1.0) / n_heads) * LOG2_E
    m_ref[...] = jnp.full(m_ref.shape, MASKED_LOGIT, F32)
    l_ref[...] = jnp.zeros(l_ref.shape, F32)
    acc_ref[...] = jnp.zeros(acc_ref.shape, F32)
    t_ids = t0 + lax.broadcasted_iota(jnp.int32, (tq, tk), 0)
    lane_ids = lax.broadcasted_iota(jnp.int32, (tq, tk), 1)
    col_ids = lax.broadcasted_iota(jnp.int32, (1, tk), 1)

    def attend_chunk(j, carry):
        s0 = pl.multiple_of(j * tk, tk)
        cb = ckv_ref[0, pl.ds(s0, tk), :]
        lg_ref[...] = lax.dot_general(qlat_ref[...], cb, NT_DIMS, preferred_element_type=F32)
        sel = (key_ref[:, pl.ds(s0, tk)] >= _lane_tile(thr, tk)) & (s0 + lane_ids <= t_ids)
        bias_ref[...] = jnp.where(sel, 0.0, MASKED_LOGIT)
        rel = (s0 - t0 + col_ids).astype(F32)

        def head(h, c2):
            rows = pl.ds(pl.multiple_of(h * tq, tq), tq)
            x = lg_ref[rows, :] + _lane_tile(slope_ref[h], tk) * rel + bias_ref[...]
            m_old = m_ref[rows, :]
            m_new = jnp.maximum(m_old, jnp.max(x, axis=1, keepdims=True))
            alpha = jnp.exp2(m_old - m_new)
            p = jnp.exp2(x - _lane_tile(m_new, tk))
            l_ref[rows, :] = alpha * l_ref[rows, :] + jnp.sum(p, axis=1, keepdims=True)
            m_ref[rows, :] = m_new
            alpha_ref[rows, :] = alpha
            p_ref[rows, :] = p.astype(BF16)
            return c2

        lax.fori_loop(0, n_heads, head, 0, unroll=True)
        lat = acc_ref.shape[1]
        acc_ref[...] = (acc_ref[...] * _lane_tile(alpha_ref[...], lat)
                        + jnp.dot(p_ref[...], cb, preferred_element_type=F32))
        return carry

    lax.fori_loop(0, n_att, attend_chunk, 0)

    ss = jnp.zeros((tq, 1), F32)
    for h in range(n_heads):
        rows = slice(h * tq, (h + 1) * tq)
        inv_l = _lane_tile(1.0 / l_ref[rows, :], acc_ref.shape[1])
        oh = jnp.dot((acc_ref[rows, :] * inv_l).astype(BF16), wuv_ref[h],
                     preferred_element_type=F32)
        obuf_ref[:, h * HEAD_DIM:(h + 1) * HEAD_DIM] = oh
        ss = ss + jnp.sum(oh * oh, axis=1, keepdims=True)
    inv = lax.rsqrt(ss * (1.0 / (n_heads * HEAD_DIM)) + EPS)
    o_ref[0] = (obuf_ref[...] * inv * g_ref[...]).astype(o_ref.dtype)


def dsa_attention(qidx, kidx, widx, qa, ckv, w_uk, w_uv, g, tq=128, tk_score=256, tk=512):
    qidx_arr, qidx_blk = qidx
    qa_arr, qa_blk = qa
    widx_arr, widx_blk = widx
    B, S, lat = ckv.shape
    n_heads = w_uk.shape[0]
    n_idx = IDX_HEADS
    tq, tk = min(tq, S), min(tk, S)
    tk_score = min(tk_score, tk)
    k_sel = min(TOPK_MAX, S // 4)
    width = n_heads * HEAD_DIM
    kernel = functools.partial(_dsa_kernel, tq=tq, tk_score=tk_score, tk=tk, k_sel=k_sel,
                               n_heads=n_heads, n_idx=n_idx)
    once = pl.Buffered(1)
    return pl.pallas_call(
        kernel,
        grid=(B, S // tq),
        in_specs=[
            pl.BlockSpec((1, tq, n_idx * IDX_DIM), lambda b, i: (b, i, qidx_blk)),
            pl.BlockSpec((1, S, IDX_DIM), lambda b, i: (b, 0, 0), pipeline_mode=once),
            pl.BlockSpec((1, tq, LANES), lambda b, i: (b, i, widx_blk)),
            pl.BlockSpec((1, tq, width), lambda b, i: (b, i, qa_blk)),
            pl.BlockSpec((1, S, lat), lambda b, i: (b, 0, 0), pipeline_mode=once),
            pl.BlockSpec((n_heads, HEAD_DIM, lat), lambda b, i: (0, 0, 0), pipeline_mode=once),
            pl.BlockSpec((n_heads, lat, HEAD_DIM), lambda b, i: (0, 0, 0), pipeline_mode=once),
            pl.BlockSpec((1, width), lambda b, i: (0, 0)),
        ],
        out_specs=pl.BlockSpec((1, tq, width), lambda b, i: (b, i, 0)),
        out_shape=jax.ShapeDtypeStruct((B, S, width), BF16),
        scratch_shapes=[
            pltpu.VMEM((tq, S), jnp.int32),
            pltpu.VMEM((n_idx, tq, LANES), F32),
            pltpu.VMEM((n_idx * tq, IDX_DIM), BF16),
            pltpu.VMEM((n_heads * tq, lat), BF16),
            pltpu.VMEM((n_heads * tq, tk), F32),
            pltpu.VMEM((tq, tk), F32),
            pltpu.VMEM((n_heads * tq, tk), BF16),
            pltpu.VMEM((n_heads * tq, lat), F32),
            pltpu.VMEM((n_heads * tq, LANES), F32),
            pltpu.VMEM((n_heads * tq, LANES), F32),
            pltpu.VMEM((n_heads * tq, LANES), F32),
            pltpu.VMEM((n_heads, 1, LANES), F32),
            pltpu.VMEM((tq, width), F32),
        ],
        compiler_params=_cparams(2),
        name="dsa_attention",
    )(qidx_arr, kidx, widx_arr, qa_arr, ckv, w_uk, w_uv, g.reshape(1, width))


def _sb_kernel(q_ref, k_ref, v_ref, o_ref, acc_ref, carry_ref, *, tq, hb):
    i = pl.program_id(2)
    t0 = i * tq
    row = lax.broadcasted_iota(jnp.int32, (tq, tq), 0)
    col = lax.broadcasted_iota(jnp.int32, (tq, tq), 1)
    suffix_ones = jnp.where(row > col, 1.0, 0.0).astype(BF16)
    acc_ref[...] = jnp.zeros(acc_ref.shape, F32)
    carry_ref[...] = jnp.zeros(carry_ref.shape, F32)

    def keep_going(state):
        j, carry_min = state
        return (j >= 0) & (carry_min <= F32_EXP_UNDERFLOW)

    def block(state):
        j, _ = state
        s0 = pl.multiple_of(j * tq, tq)
        strict = (s0 + col) < (t0 + row)
        carry_min = None
        for h in range(hb):
            cols = slice(h * HEAD_DIM, (h + 1) * HEAD_DIM)
            kb = k_ref[0, pl.ds(s0, tq), cols]
            vb = v_ref[0, pl.ds(s0, tq), cols]
            z = lax.dot_general(q_ref[0, :, cols], kb, NT_DIMS, preferred_element_type=F32) * ATTN_SCALE
            sp = jnp.maximum(z, 0.0) + jnp.log(1.0 + jnp.exp(-jnp.abs(z)))
            u = jnp.where(strict, sp, 0.0)
            u_hi, u_lo = _split_bf16(u)
            within = (jnp.dot(u_hi, suffix_ones, preferred_element_type=F32)
                      + jnp.dot(u_lo, suffix_ones, preferred_element_type=F32))
            carry = carry_ref[:, cols]
            a = jnp.where(strict, jnp.exp(z - sp - (_lane_tile(carry, tq) + within)), 0.0)
            acc_ref[:, cols] += jnp.dot(a.astype(BF16), vb, preferred_element_type=F32)
            carry = carry + jnp.sum(u, axis=1, keepdims=True)
            carry_ref[:, cols] = carry
            head_min = jnp.min(carry)
            carry_min = head_min if carry_min is None else jnp.minimum(carry_min, head_min)
        return j - 1, carry_min

    lax.while_loop(keep_going, block, (i, jnp.float32(0.0)))
    o_ref[0] = acc_ref[...]


def stick_breaking_attention(q, k, v, n_heads, tq=256, hb=4):
    (q_arr, q_blk), (k_arr, k_blk), (v_arr, v_blk) = q, k, v
    B, S, _ = q_arr.shape
    tq = min(tq, S)
    hb = min(hb, n_heads)
    wb = hb * HEAD_DIM
    return pl.pallas_call(
        functools.partial(_sb_kernel, tq=tq, hb=hb),
        grid=(B, n_heads // hb, S // tq),
        in_specs=[pl.BlockSpec((1, tq, wb), lambda b, h, i: (b, i, q_blk + h)),
                  pl.BlockSpec((1, S, wb), lambda b, h, i: (b, 0, k_blk + h)),
                  pl.BlockSpec((1, S, wb), lambda b, h, i: (b, 0, v_blk + h))],
        out_specs=pl.BlockSpec((1, tq, wb), lambda b, h, i: (b, i, h)),
        out_shape=jax.ShapeDtypeStruct((B, S, n_heads * HEAD_DIM), F32),
        scratch_shapes=[pltpu.VMEM((tq, wb), F32), pltpu.VMEM((tq, wb), F32)],
        compiler_params=_cparams(3),
        name="stick_breaking_attention",
    )(q_arr, k_arr, v_arr)


def _layer_norm(v, g, b):
    mu = jnp.mean(v, axis=-1, keepdims=True)
    d = v - mu
    var = jnp.mean(d * d, axis=-1, keepdims=True)
    return d * lax.rsqrt(var + EPS) * g + b


def _outproj_ln_kernel(od_ref, os_ref, gsb_ref, w_ref, x_ref, gate_ref, g_ref, b_ref, sc_ref, sh_ref,
                       x1_ref, h2_ref, a_ref, mix_ref, *, tn):
    j = pl.program_id(1)
    wd = od_ref.shape[1]

    @pl.when(j == 0)
    def _():
        a_ref[:, :wd] = od_ref[...]
        o = os_ref[...]
        ms = jnp.mean(o * o, axis=-1, keepdims=True)
        a_ref[:, wd:] = (o * lax.rsqrt(ms + EPS) * gsb_ref[...]).astype(BF16)

    mix_ref[:, pl.ds(pl.multiple_of(j * tn, tn), tn)] = jnp.dot(
        a_ref[...], w_ref[...], preferred_element_type=F32)

    @pl.when(j == pl.num_programs(1) - 1)
    def _():
        y = _layer_norm(DEEPNORM_ALPHA * x_ref[...] + gate_ref[0] * mix_ref[...], g_ref[...], b_ref[...])
        x1_ref[...] = y
        h2_ref[...] = (y * (1.0 + sc_ref[0]) + sh_ref[0]).astype(BF16)


def outproj_ln(o_dsa, o_sb, g_sb, w_out, x, gate, ln_g, ln_b, scale, shift, seq_len, tm=256, tn=512):
    M, D = x.shape
    wd, ws = o_dsa.shape[1], o_sb.shape[1]
    B = gate.shape[0]
    tm, tn = min(tm, seq_len), min(tn, D)

    def per_batch(i, j):
        return (i * tm // seq_len, 0, 0)

    row = lambda i, j: (i, 0)
    shared = lambda i, j: (0, 0)
    return pl.pallas_call(
        functools.partial(_outproj_ln_kernel, tn=tn),
        grid=(M // tm, D // tn),
        in_specs=[
            pl.BlockSpec((tm, wd), row),
            pl.BlockSpec((tm, ws), row),
            pl.BlockSpec((1, ws), shared),
            pl.BlockSpec((wd + ws, tn), lambda i, j: (0, j)),
            pl.BlockSpec((tm, D), row),
            pl.BlockSpec((1, 1, D), per_batch),
            pl.BlockSpec((1, D), shared),
            pl.BlockSpec((1, D), shared),
            pl.BlockSpec((1, 1, D), per_batch),
            pl.BlockSpec((1, 1, D), per_batch),
        ],
        out_specs=[pl.BlockSpec((tm, D), row), pl.BlockSpec((tm, D), row)],
        out_shape=[jax.ShapeDtypeStruct((M, D), F32), jax.ShapeDtypeStruct((M, D), BF16)],
        scratch_shapes=[pltpu.VMEM((tm, wd + ws), BF16), pltpu.VMEM((tm, D), F32)],
        compiler_params=_cparams(2),
        name="outproj_ln",
    )(o_dsa, o_sb, g_sb.reshape(1, ws), w_out, x, gate.reshape(B, 1, D), ln_g.reshape(1, D),
      ln_b.reshape(1, D), scale.reshape(B, 1, D), shift.reshape(B, 1, D))


def _scatter_rows_kernel(pos_ref, pad_end_ref, cnt_ref, h_ref, xg_ref, zero_ref, sem, *, tm, tt, top_k,
                         n_exp):
    i = pl.program_id(0)

    def fill_copy(e):
        start = pl.multiple_of(pad_end_ref[e] - tm, tm)
        return pltpu.make_async_copy(zero_ref, xg_ref.at[pl.ds(start, tm)], sem.at[0])

    @pl.when(i == 0)
    def _():
        zero_ref[...] = jnp.zeros(zero_ref.shape, zero_ref.dtype)

        def start_fill(e, c):
            @pl.when(cnt_ref[e] % tm != 0)
            def _():
                fill_copy(e).start()
            return c

        def wait_fill(e, c):
            @pl.when(cnt_ref[e] % tm != 0)
            def _():
                fill_copy(e).wait()
            return c

        def tail_copy(r):
            return pltpu.make_async_copy(
                zero_ref, xg_ref.at[pl.ds(pl.multiple_of(r * tm, tm), tm)], sem.at[0])

        def start_tail(r, c):
            tail_copy(r).start()
            return c

        def wait_tail(r, c):
            tail_copy(r).wait()
            return c

        n_used = pad_end_ref[n_exp - 1] // tm
        n_blocks = xg_ref.shape[0] // tm
        lax.fori_loop(0, n_exp, start_fill, 0)
        lax.fori_loop(n_used, n_blocks, start_tail, 0)
        lax.fori_loop(0, n_exp, wait_fill, 0)
        lax.fori_loop(n_used, n_blocks, wait_tail, 0)

    base = i * tt

    def row_copy(t, k):
        tok = base + t
        return pltpu.make_async_copy(h_ref.at[tok], xg_ref.at[pos_ref[tok * top_k + k]], sem.at[1])

    def start_rows(t, c):
        for k in range(top_k):
            row_copy(t, k).start()
        return c

    def wait_rows(t, c):
        for k in range(top_k):
            row_copy(t, k).wait()
        return c

    lax.fori_loop(0, tt, start_rows, 0)
    lax.fori_loop(0, tt, wait_rows, 0)


def scatter_rows(pos, pad_end, counts, h_slabs, n_rows, tm, tt=256):
    n_tok, P, _ = h_slabs.shape
    dtype = h_slabs.dtype
    tt = min(tt, n_tok)
    kernel = functools.partial(_scatter_rows_kernel, tm=tm, tt=tt, top_k=TOP_K_EXPERTS, n_exp=N_EXPERTS)
    return pl.pallas_call(
        kernel,
        grid_spec=pltpu.PrefetchScalarGridSpec(
            num_scalar_prefetch=3,
            grid=(n_tok // tt,),
            in_specs=[pl.BlockSpec(memory_space=pl.ANY)],
            out_specs=pl.BlockSpec(memory_space=pl.ANY),
            scratch_shapes=[pltpu.VMEM((tm, P, LANES), dtype), pltpu.SemaphoreType.DMA((2,))],
        ),
        out_shape=jax.ShapeDtypeStruct((n_rows, P, LANES), dtype),
        compiler_params=_cparams(1),
        name="scatter_rows",
    )(pos, pad_end, counts, h_slabs)


def _expert_changed(blk_e_ref, r):
    return (r == 0) | (blk_e_ref[r] != blk_e_ref[jnp.maximum(r - 1, 0)])


def _moe_up_kernel(blk_e_ref, n_used_ref, x_ref, wg_ref, wl_ref, bg_ref, bl_ref, o_ref, wg_bf, wl_bf):
    r = pl.program_id(1)
    used = r < n_used_ref[0]

    @pl.when(used & _expert_changed(blk_e_ref, r))
    def _():
        wg_bf[...] = wg_ref[0].astype(BF16)
        wl_bf[...] = wl_ref[0].astype(BF16)

    @pl.when(used)
    def _():
        x = x_ref[...]
        glu = jnp.dot(x, wg_bf[...], preferred_element_type=F32) + bg_ref[0]
        lin = jnp.dot(x, wl_bf[...], preferred_element_type=F32) + bl_ref[0]
        glu = jnp.minimum(glu, SWIGLU_LIMIT)
        lin = jnp.clip(lin, -SWIGLU_LIMIT, SWIGLU_LIMIT)
        act = glu * (1.0 / (1.0 + jnp.exp(-SWIGLU_ALPHA * glu))) * (lin + 1.0)
        o_ref[...] = act.astype(o_ref.dtype)

    @pl.when(jnp.logical_not(used))
    def _():
        o_ref[...] = jnp.zeros(o_ref.shape, o_ref.dtype)


def _moe_down_kernel(blk_e_ref, n_used_ref, a_ref, wd_ref, bd_ref, o_ref, wd_bf):
    r = pl.program_id(1)
    used = r < n_used_ref[0]

    @pl.when(used & _expert_changed(blk_e_ref, r))
    def _():
        wd_bf[...] = wd_ref[0].astype(BF16)

    @pl.when(used)
    def _():
        out = jnp.dot(a_ref[...], wd_bf[...], preferred_element_type=F32) + bd_ref[0]
        for s in range(o_ref.shape[1]):
            o_ref[:, s, :] = out[:, s * LANES:(s + 1) * LANES]

    @pl.when(jnp.logical_not(used))
    def _():
        o_ref[...] = jnp.zeros(o_ref.shape, o_ref.dtype)


def moe_experts(xg, blk_e, n_used, w_up, b_up, w_down, b_down, tm, tn_up=512, tn_down=1024):
    R, D = xg.shape
    n_exp, d_exp = w_down.shape[0], w_down.shape[1]
    tn_up, tn_down = min(tn_up, d_exp), min(tn_down, D)
    n_up = d_exp // tn_up

    def row(r, n):
        return jnp.minimum(r, n[0] - 1)

    def expert(r, e, n):
        return e[row(r, n)]

    b_up3 = b_up.reshape(n_exp, 1, 2 * d_exp)
    act = pl.pallas_call(
        _moe_up_kernel,
        grid_spec=pltpu.PrefetchScalarGridSpec(
            num_scalar_prefetch=2,
            grid=(n_up, R // tm),
            in_specs=[
                pl.BlockSpec((tm, D), lambda c, r, e, n: (row(r, n), 0)),
                pl.BlockSpec((1, D, tn_up), lambda c, r, e, n: (expert(r, e, n), 0, c)),
                pl.BlockSpec((1, D, tn_up), lambda c, r, e, n: (expert(r, e, n), 0, n_up + c)),
                pl.BlockSpec((1, 1, tn_up), lambda c, r, e, n: (expert(r, e, n), 0, c)),
                pl.BlockSpec((1, 1, tn_up), lambda c, r, e, n: (expert(r, e, n), 0, n_up + c)),
            ],
            out_specs=pl.BlockSpec((tm, tn_up), lambda c, r, e, n: (r, c)),
            scratch_shapes=[pltpu.VMEM((D, tn_up), BF16), pltpu.VMEM((D, tn_up), BF16)],
        ),
        out_shape=jax.ShapeDtypeStruct((R, d_exp), BF16),
        compiler_params=_cparams(2),
        name="moe_up",
    )(blk_e, n_used, xg, w_up, w_up, b_up3, b_up3)
    return pl.pallas_call(
        _moe_down_kernel,
        grid_spec=pltpu.PrefetchScalarGridSpec(
            num_scalar_prefetch=2,
            grid=(D // tn_down, R // tm),
            in_specs=[
                pl.BlockSpec((tm, d_exp), lambda c, r, e, n: (row(r, n), 0)),
                pl.BlockSpec((1, d_exp, tn_down), lambda c, r, e, n: (expert(r, e, n), 0, c)),
                pl.BlockSpec((1, 1, tn_down), lambda c, r, e, n: (expert(r, e, n), 0, c)),
            ],
            out_specs=pl.BlockSpec((tm, tn_down // LANES, LANES), lambda c, r, e, n: (r, c, 0)),
            scratch_shapes=[pltpu.VMEM((d_exp, tn_down), BF16)],
        ),
        out_shape=jax.ShapeDtypeStruct((R, D // LANES, LANES), F32),
        compiler_params=_cparams(2),
        name="moe_down",
    )(blk_e, n_used, act, w_down, b_down.reshape(n_exp, 1, D))


def _combine_ln_kernel(pos_ref, gates_ref, rows_ref, x_ref, gate_ref, g_ref, b_ref, o_ref,
                       buf0, buf1, ysl_ref, y_ref, sem, *, tt, top_k):
    i = pl.program_id(0)
    n_blocks = pl.num_programs(0)

    def row_copy(blk, buf, slot, t, k):
        src = rows_ref.at[pos_ref[(blk * tt + t) * top_k + k]]
        return pltpu.make_async_copy(src, buf.at[k * tt + t], sem.at[slot])

    def start_block(blk, buf, slot):
        def body(t, c):
            for k in range(top_k):
                row_copy(blk, buf, slot, t, k).start()
            return c
        lax.fori_loop(0, tt, body, 0)

    def wait_block(blk, buf, slot):
        def body(t, c):
            for k in range(top_k):
                row_copy(blk, buf, slot, t, k).wait()
            return c
        lax.fori_loop(0, tt, body, 0)

    def combine(buf):
        def per_token(t, c):
            base = (i * tt + t) * top_k
            acc = gates_ref[base] * buf[t]
            for k in range(1, top_k):
                acc = acc + gates_ref[base + k] * buf[k * tt + t]
            ysl_ref[t] = acc
            return c
        lax.fori_loop(0, tt, per_token, 0, unroll=4)
        for s in range(buf.shape[1]):
            y_ref[:, s * LANES:(s + 1) * LANES] = ysl_ref[:, s, :]

    even = i % 2 == 0

    @pl.when(i == 0)
    def _():
        start_block(0, buf0, 0)

    @pl.when((i + 1 < n_blocks) & even)
    def _():
        start_block(i + 1, buf1, 1)

    @pl.when((i + 1 < n_blocks) & jnp.logical_not(even))
    def _():
        start_block(i + 1, buf0, 0)

    @pl.when(even)
    def _():
        wait_block(i, buf0, 0)
        combine(buf0)

    @pl.when(jnp.logical_not(even))
    def _():
        wait_block(i, buf1, 1)
        combine(buf1)

    o_ref[...] = _layer_norm(DEEPNORM_ALPHA * x_ref[...] + gate_ref[0] * y_ref[...], g_ref[...], b_ref[...])


def combine_ln(pos, gates, out_rows, x, gate, ln_g, ln_b, seq_len, tt=128):
    n_tok, D = x.shape
    B = gate.shape[0]
    tt = min(tt, seq_len)
    P = out_rows.shape[1]
    kernel = functools.partial(_combine_ln_kernel, tt=tt, top_k=TOP_K_EXPERTS)
    return pl.pallas_call(
        kernel,
        grid_spec=pltpu.PrefetchScalarGridSpec(
            num_scalar_prefetch=2,
            grid=(n_tok // tt,),
            in_specs=[
                pl.BlockSpec(memory_space=pl.ANY),
                pl.BlockSpec((tt, D), lambda i, p, q: (i, 0)),
                pl.BlockSpec((1, 1, D), lambda i, p, q: (i * tt // seq_len, 0, 0)),
                pl.BlockSpec((1, D), lambda i, p, q: (0, 0)),
                pl.BlockSpec((1, D), lambda i, p, q: (0, 0)),
            ],
            out_specs=pl.BlockSpec((tt, D), lambda i, p, q: (i, 0)),
            scratch_shapes=[pltpu.VMEM((TOP_K_EXPERTS * tt, P, LANES), F32),
                            pltpu.VMEM((TOP_K_EXPERTS * tt, P, LANES), F32),
                            pltpu.VMEM((tt, P, LANES), F32),
                            pltpu.VMEM((tt, D), F32),
                            pltpu.SemaphoreType.DMA((2,))],
        ),
        out_shape=jax.ShapeDtypeStruct((n_tok, D), F32),
        compiler_params=_cparams(1),
        name="combine_ln",
    )(pos, gates, out_rows, x, gate.reshape(B, 1, D), ln_g.reshape(1, D), ln_b.reshape(1, D))


def route(logits, tm):
    n_tok = logits.shape[0]
    top_logit, top_e = lax.top_k(logits, TOP_K_EXPERTS)
    gates = jax.nn.softmax(top_logit, axis=-1)
    n_asg = n_tok * TOP_K_EXPERTS
    e_flat = top_e.reshape(n_asg)
    onehot = (e_flat[:, None] == jnp.arange(N_EXPERTS, dtype=e_flat.dtype)[None, :]).astype(jnp.int32)
    running = jnp.cumsum(onehot, axis=0)
    rank = jnp.take_along_axis(running, e_flat[:, None], axis=1)[:, 0] - 1
    counts = running[-1]
    padded = (counts + tm - 1) // tm * tm
    pad_end = jnp.cumsum(padded)
    pad_start = pad_end - padded
    pos = (pad_start[e_flat] + rank).astype(jnp.int32)
    n_blocks = -(-n_asg // tm) + N_EXPERTS
    blk_e = jnp.minimum(jnp.searchsorted(pad_end, jnp.arange(n_blocks) * tm, side='right'),
                        N_EXPERTS - 1).astype(jnp.int32)
    n_used = (pad_end[-1] // tm).astype(jnp.int32).reshape(1)
    return (gates, pos, blk_e, n_used, pad_end.astype(jnp.int32), counts.astype(jnp.int32),
            n_blocks * tm)


def router_logits(h2, w_router, b_router):
    D = h2.shape[1]
    wr = jnp.zeros((D, LANES), BF16).at[:, :N_EXPERTS].set(w_router.astype(BF16))
    return matmul(h2, wr, F32, tm=512, tn=LANES)[:, :N_EXPERTS] + b_router.astype(F32)


def moe_block(h2, x1, gate_f, ln_g, ln_b, w_router, b_router, w_up, b_up, w_down, b_down, seq_len, tm=512):
    n_tok, D = h2.shape
    logits = router_logits(h2, w_router, b_router)
    gates, pos, blk_e, n_used, pad_end, counts, n_rows = route(logits, tm)
    xg = scatter_rows(pos, pad_end, counts, h2.reshape(n_tok, D // LANES, LANES), n_rows, tm)
    out_rows = moe_experts(xg.reshape(n_rows, D), blk_e, n_used, w_up, b_up, w_down, b_down, tm)
    return combine_ln(pos, gates.reshape(-1), out_rows, x1, gate_f, ln_g, ln_b, seq_len)


def _layer(x, c, w_ada, b_ada, w_in, kv_norm_g, w_uk, w_uv, grp_norm_dsa, grp_norm_sb, w_out,
           ln1_g, ln1_b, w_router, b_router, w_up, b_up, w_down, b_down, ln2_g, ln2_b):
    B, S, D = x.shape
    n_tok = B * S
    n_dsa = w_uk.shape[0]
    w_qa = n_dsa * HEAD_DIM
    w_qidx = IDX_HEADS * IDX_DIM
    w_sb = grp_norm_sb.shape[0]
    n_sb = w_sb // HEAD_DIM

    mod = ada_modulation(c, w_ada, b_ada)
    shift_a, scale_a, gate_a, shift_f, scale_f, gate_f = jnp.split(mod, 6, axis=-1)

    o1 = w_qa
    o2 = o1 + KV_LATENT
    o3 = o2 + w_qidx
    o4 = o3 + IDX_DIM
    o5 = o4 + IDX_HEADS
    w_main = jnp.concatenate([w_in[:, o2:o3], w_in[:, :o1], w_in[:, o5:]], axis=1).astype(BF16)
    n_small = KV_LATENT + IDX_DIM + IDX_HEADS
    n_small_pad = KV_LATENT + IDX_DIM + LANES
    w_small = jnp.concatenate(
        [w_in[:, o1:o2], w_in[:, o3:o5], jnp.zeros((D, n_small_pad - n_small), F32)], axis=1).astype(BF16)

    h = modulate(x, scale_a, shift_a).reshape(n_tok, D)
    proj = matmul(h, w_main, BF16, tm=512, tn=512).reshape(B, S, w_main.shape[1])
    small = matmul(h, w_small, F32, tm=512, tn=n_small_pad)
    ckv, kidx = kv_prep(small, kv_norm_g)

    sb_w = 4 * HEAD_DIM
    o_dsa = dsa_attention(
        qidx=(proj, 0), kidx=kidx.reshape(B, S, IDX_DIM),
        widx=(small.reshape(B, S, n_small_pad), (KV_LATENT + IDX_DIM) // LANES),
        qa=(proj, w_qidx // w_qa), ckv=ckv.reshape(B, S, KV_LATENT),
        w_uk=w_uk.astype(BF16), w_uv=w_uv.astype(BF16), g=grp_norm_dsa)
    sb0 = w_qidx + w_qa
    o_sb = stick_breaking_attention((proj, sb0 // sb_w), (proj, (sb0 + w_sb) // sb_w),
                                    (proj, (sb0 + 2 * w_sb) // sb_w), n_sb)

    x1, h2 = outproj_ln(o_dsa.reshape(n_tok, w_qa), o_sb.reshape(n_tok, w_sb), grp_norm_sb,
                        w_out.astype(BF16), x.reshape(n_tok, D), gate_a, ln1_g, ln1_b, scale_f, shift_f, S)
    out = moe_block(h2, x1, gate_f, ln2_g, ln2_b, w_router, b_router, w_up, b_up, w_down, b_down, S)
    return out.reshape(B, S, D)


def kernel(x, c, w_ada, b_ada, w_in, kv_norm_g, w_uk, w_uv, grp_norm_dsa, grp_norm_sb, w_out, ln1_g, ln1_b, w_router, b_router, w_up, b_up, w_down, b_down, ln2_g, ln2_b):
    return _layer(x, c, w_ada[0], b_ada[0], w_in[0], kv_norm_g[0], w_uk[0], w_uv[0],
                  grp_norm_dsa[0], grp_norm_sb[0], w_out[0], ln1_g[0], ln1_b[0], w_router[0],
                  b_router[0], w_up[0], b_up[0], w_down[0], b_down[0], ln2_g[0], ln2_b[0])
```

```python
import functools
import math

import jax
import jax.numpy as jnp
from jax import lax
from jax.experimental import pallas as pl
from jax.experimental.pallas import tpu as pltpu

F32 = jnp.float32
BF16 = jnp.bfloat16

HEAD_DIM = 128
KV_LATENT = 512
IDX_HEADS = 32
IDX_DIM = 128
TOPK_MAX = 256
N_EXPERTS = 32
TOP_K_EXPERTS = 4
SWIGLU_ALPHA = 1.702
SWIGLU_LIMIT = 7.0
DEPTH = 1
DEEPNORM_ALPHA = (2 * DEPTH) ** 0.25
EPS = 1e-5
INDEX_SCALE = (IDX_HEADS * IDX_DIM) ** -0.5
ATTN_SCALE = HEAD_DIM ** -0.5
LOG2_E = math.log2(math.e)

LANES = 128
VMEM_LIMIT_BYTES = 56 * 1024 * 1024

MASKED_LOGIT = -1e30
F32_EXP_UNDERFLOW = 104.0

NT_DIMS = (((1,), (1,)), ((), ()))


def _cparams(n_axes):
    return pltpu.CompilerParams(
        dimension_semantics=("arbitrary",) * n_axes, vmem_limit_bytes=VMEM_LIMIT_BYTES)


def _split_bf16(v):
    hi = v.astype(BF16)
    lo = (v - hi.astype(F32)).astype(BF16)
    return hi, lo


def _lane_tile(v, width):
    return jnp.concatenate([v] * (width // LANES), axis=1)


def _ada_kernel(c_ref, w_ref, b_ref, o_ref):
    c = c_ref[...]
    s = c * (1.0 / (1.0 + jnp.exp(-c)))
    s_hi, s_lo = _split_bf16(s)
    w_hi, w_lo = _split_bf16(w_ref[...])
    acc = jnp.dot(s_hi, w_hi, preferred_element_type=F32)
    acc += jnp.dot(s_hi, w_lo, preferred_element_type=F32)
    acc += jnp.dot(s_lo, w_hi, preferred_element_type=F32)
    o_ref[...] = acc + b_ref[...]


def ada_modulation(c, w_ada, b_ada, tn=512):
    B, D = c.shape
    N = w_ada.shape[1]
    rows = 8
    c_pad = jnp.zeros((rows, D), F32).at[:B].set(c)
    out = pl.pallas_call(
        _ada_kernel,
        grid=(N // tn,),
        in_specs=[pl.BlockSpec((rows, D), lambda j: (0, 0)),
                  pl.BlockSpec((D, tn), lambda j: (0, j)),
                  pl.BlockSpec((1, tn), lambda j: (0, j))],
        out_specs=pl.BlockSpec((rows, tn), lambda j: (0, j)),
        out_shape=jax.ShapeDtypeStruct((rows, N), F32),
        compiler_params=_cparams(1),
        name="ada_modulation",
    )(c_pad, w_ada, b_ada.reshape(1, N))
    return out[:B]


def _modulate_kernel(x_ref, sc_ref, sh_ref, o_ref):
    o_ref[0] = (x_ref[0] * (1.0 + sc_ref[0]) + sh_ref[0]).astype(o_ref.dtype)


def modulate(x, scale, shift, tm=512):
    B, S, D = x.shape
    tm = min(tm, S)
    vec = pl.BlockSpec((1, 1, D), lambda b, i: (b, 0, 0))
    return pl.pallas_call(
        _modulate_kernel,
        grid=(B, S // tm),
        in_specs=[pl.BlockSpec((1, tm, D), lambda b, i: (b, i, 0)), vec, vec],
        out_specs=pl.BlockSpec((1, tm, D), lambda b, i: (b, i, 0)),
        out_shape=jax.ShapeDtypeStruct((B, S, D), BF16),
        compiler_params=_cparams(2),
        name="modulate",
    )(x, scale.reshape(B, 1, D), shift.reshape(B, 1, D))


def _matmul_kernel(a_ref, b_ref, o_ref):
    o_ref[...] = jnp.dot(a_ref[...], b_ref[...], preferred_element_type=F32).astype(o_ref.dtype)


def matmul(a, b, out_dtype, tm=512, tn=512):
    M, K = a.shape
    N = b.shape[1]
    tm, tn = min(tm, M), min(tn, N)
    return pl.pallas_call(
        _matmul_kernel,
        grid=(M // tm, N // tn),
        in_specs=[pl.BlockSpec((tm, K), lambda i, j: (i, 0)),
                  pl.BlockSpec((K, tn), lambda i, j: (0, j))],
        out_specs=pl.BlockSpec((tm, tn), lambda i, j: (i, j)),
        out_shape=jax.ShapeDtypeStruct((M, N), out_dtype),
        compiler_params=_cparams(2),
        name="matmul",
    )(a, b)


def _kv_prep_kernel(s_ref, g_ref, ckv_ref, kidx_ref):
    x = s_ref[:, :KV_LATENT]
    ms = jnp.mean(x * x, axis=-1, keepdims=True)
    ckv_ref[...] = (x * lax.rsqrt(ms + EPS) * g_ref[...]).astype(BF16)
    kidx_ref[...] = s_ref[:, KV_LATENT:KV_LATENT + IDX_DIM].astype(BF16)


def kv_prep(small, g, tm=512):
    M, W = small.shape
    tm = min(tm, M)
    return pl.pallas_call(
        _kv_prep_kernel,
        grid=(M // tm,),
        in_specs=[pl.BlockSpec((tm, W), lambda i: (i, 0)), pl.BlockSpec((1, KV_LATENT), lambda i: (0, 0))],
        out_specs=[pl.BlockSpec((tm, KV_LATENT), lambda i: (i, 0)), pl.BlockSpec((tm, IDX_DIM), lambda i: (i, 0))],
        out_shape=[jax.ShapeDtypeStruct((M, KV_LATENT), BF16), jax.ShapeDtypeStruct((M, IDX_DIM), BF16)],
        compiler_params=_cparams(1),
        name="kv_prep",
    )(small, g.reshape(1, KV_LATENT))


def _sortable_key(v):
    bits = pltpu.bitcast(v + 0.0, jnp.int32)
    return bits ^ (lax.shift_right_arithmetic(bits, 31) & jnp.int32(0x7FFFFFFF))


def _dsa_kernel(qidx_ref, kidx_ref, widx_ref, qa_ref, ckv_ref, wuk_ref, wuv_ref, g_ref, o_ref,
                key_ref, wb_ref, q2_ref, qlat_ref, lg_ref, bias_ref, p_ref, acc_ref, m_ref, l_ref,
                alpha_ref, slope_ref, obuf_ref, *, tq, tk_score, tk, k_sel, n_heads, n_idx):
    i = pl.program_id(1)
    t0 = i * tq
    n_att = (t0 + tq - 1) // tk + 1
    n_score = n_att * (tk // tk_score)

    w = widx_ref[0]
    for h in range(n_idx):
        wb_ref[h] = jnp.broadcast_to(w[:, h:h + 1], (tq, LANES))
        q2_ref[h * tq:(h + 1) * tq, :] = qidx_ref[0, :, h * IDX_DIM:(h + 1) * IDX_DIM]
    t_ids_s = t0 + lax.broadcasted_iota(jnp.int32, (tq, tk_score), 0)
    lane_ids_s = lax.broadcasted_iota(jnp.int32, (tq, tk_score), 1)

    def score_chunk(j, carry):
        s0 = pl.multiple_of(j * tk_score, tk_score)
        kb = kidx_ref[0, pl.ds(s0, tk_score), :]
        lg = lax.dot_general(q2_ref[...], kb, NT_DIMS, preferred_element_type=F32)
        r = jnp.maximum(lg, 0.0).reshape(n_idx, tq, tk_score)
        parts = [jnp.sum(r[:, :, c * LANES:(c + 1) * LANES] * wb_ref[...], axis=0)
                 for c in range(tk_score // LANES)]
        sc = jnp.concatenate(parts, axis=1) * INDEX_SCALE
        sc = jnp.where(s0 + lane_ids_s <= t_ids_s, sc, -jnp.inf)
        key_ref[:, pl.ds(s0, tk_score)] = _sortable_key(sc)
        return carry

    lax.fori_loop(0, n_score, score_chunk, 0)

    def bisect(it, prefix):
        cand = prefix + lax.shift_left(jnp.int32(1), 31 - it)

        def count_chunk(j, cnt):
            s0 = pl.multiple_of(j * tk, tk)
            keys = key_ref[:, pl.ds(s0, tk)]
            for c in range(tk // LANES):
                cnt = cnt + jnp.where(keys[:, c * LANES:(c + 1) * LANES] >= cand, 1.0, 0.0)
            return cnt

        cnt = lax.fori_loop(0, n_att, count_chunk, jnp.zeros((tq, LANES), F32))
        total = jnp.sum(cnt, axis=1, keepdims=True)
        return jnp.where(total >= float(k_sel), cand, prefix)

    thr = lax.fori_loop(0, 32, bisect, jnp.full((tq, LANES), -2 ** 31, jnp.int32))

    for h in range(n_heads):
        ql = jnp.dot(qa_ref[0, :, h * HEAD_DIM:(h + 1) * HEAD_DIM], wuk_ref[h], preferred_element_type=F32)
        qlat_ref[h * tq:(h + 1) * tq, :] = (ql * (ATTN_SCALE * LOG2_E)).astype(BF16)
    head_no = lax.broadcasted_iota(jnp.int32, slope_ref.shape, 0).astype(F32)
    slope_ref[...] = jnp.exp2(-8.0 * (head_no + 1.0) / n_heads) * LOG2_E
    m_ref[...] = jnp.full(m_ref.shape, MASKED_LOGIT, F32)
    l_ref[...] = jnp.zeros(l_ref.shape, F32)
    acc_ref[...] = jnp.zeros(acc_ref.shape, F32)
    t_ids = t0 + lax.broadcasted_iota(jnp.int32, (tq, tk), 0)
    lane_ids = lax.broadcasted_iota(jnp.int32, (tq, tk), 1)
    col_ids = lax.broadcasted_iota(jnp.int32, (1, tk), 1)

    def attend_chunk(j, carry):
        s0 = pl.multiple_of(j * tk, tk)
        cb = ckv_ref[0, pl.ds(s0, tk), :]
        lg_ref[...] = lax.dot_general(qlat_ref[...], cb, NT_DIMS, preferred_element_type=F32)
        sel = (key_ref[:, pl.ds(s0, tk)] >= _lane_tile(thr, tk)) & (s0 + lane_ids <= t_ids)
        bias_ref[...] = jnp.where(sel, 0.0, MASKED_LOGIT)
        rel = (s0 - t0 + col_ids).astype(F32)

        def head(h, c2):
            rows = pl.ds(pl.multiple_of(h * tq, tq), tq)
            x = lg_ref[rows, :] + _lane_tile(slope_ref[h], tk) * rel + bias_ref[...]
            m_old = m_ref[rows, :]
            m_new = jnp.maximum(m_old, jnp.max(x, axis=1, keepdims=True))
            alpha = jnp.exp2(m_old - m_new)
            p = jnp.exp2(x - _lane_tile(m_new, tk))
            l_ref[rows, :] = alpha * l_ref[rows, :] + jnp.sum(p, axis=1, keepdims=True)
            m_ref[rows, :] = m_new
            alpha_ref[rows, :] = alpha
            p_ref[rows, :] = p.astype(BF16)
            return c2

        lax.fori_loop(0, n_heads, head, 0, unroll=True)
        lat = acc_ref.shape[1]
        acc_ref[...] = (acc_ref[...] * _lane_tile(alpha_ref[...], lat)
                        + jnp.dot(p_ref[...], cb, preferred_element_type=F32))
        return carry

    lax.fori_loop(0, n_att, attend_chunk, 0)

    ss = jnp.zeros((tq, 1), F32)
    for h in range(n_heads):
        rows = slice(h * tq, (h + 1) * tq)
        inv_l = _lane_tile(1.0 / l_ref[rows, :], acc_ref.shape[1])
        oh = jnp.dot((acc_ref[rows, :] * inv_l).astype(BF16), wuv_ref[h],
                     preferred_element_type=F32)
        obuf_ref[:, h * HEAD_DIM:(h + 1) * HEAD_DIM] = oh
        ss = ss + jnp.sum(oh * oh, axis=1, keepdims=True)
    inv = lax.rsqrt(ss * (1.0 / (n_heads * HEAD_DIM)) + EPS)
    o_ref[0] = (obuf_ref[...] * inv * g_ref[...]).astype(o_ref.dtype)


def dsa_attention(qidx, kidx, widx, qa, ckv, w_uk, w_uv, g, tq=128, tk_score=512, tk=512):
    qidx_arr, qidx_blk = qidx
    qa_arr, qa_blk = qa
    widx_arr, widx_blk = widx
    B, S, lat = ckv.shape
    n_heads = w_uk.shape[0]
    n_idx = IDX_HEADS
    tq, tk = min(tq, S), min(tk, S)
    tk_score = min(tk_score, tk)
    k_sel = min(TOPK_MAX, S // 4)
    width = n_heads * HEAD_DIM
    kernel = functools.partial(_dsa_kernel, tq=tq, tk_score=tk_score, tk=tk, k_sel=k_sel,
                               n_heads=n_heads, n_idx=n_idx)
    once = pl.Buffered(1)
    return pl.pallas_call(
        kernel,
        grid=(B, S // tq),
        in_specs=[
            pl.BlockSpec((1, tq, n_idx * IDX_DIM), lambda b, i: (b, i, qidx_blk)),
            pl.BlockSpec((1, S, IDX_DIM), lambda b, i: (b, 0, 0), pipeline_mode=once),
            pl.BlockSpec((1, tq, LANES), lambda b, i: (b, i, widx_blk)),
            pl.BlockSpec((1, tq, width), lambda b, i: (b, i, qa_blk)),
            pl.BlockSpec((1, S, lat), lambda b, i: (b, 0, 0), pipeline_mode=once),
            pl.BlockSpec((n_heads, HEAD_DIM, lat), lambda b, i: (0, 0, 0), pipeline_mode=once),
            pl.BlockSpec((n_heads, lat, HEAD_DIM), lambda b, i: (0, 0, 0), pipeline_mode=once),
            pl.BlockSpec((1, width), lambda b, i: (0, 0)),
        ],
        out_specs=pl.BlockSpec((1, tq, width), lambda b, i: (b, i, 0)),
        out_shape=jax.ShapeDtypeStruct((B, S, width), BF16),
        scratch_shapes=[
            pltpu.VMEM((tq, S), jnp.int32),
            pltpu.VMEM((n_idx, tq, LANES), F32),
            pltpu.VMEM((n_idx * tq, IDX_DIM), BF16),
            pltpu.VMEM((n_heads * tq, lat), BF16),
            pltpu.VMEM((n_heads * tq, tk), F32),
            pltpu.VMEM((tq, tk), F32),
            pltpu.VMEM((n_heads * tq, tk), BF16),
            pltpu.VMEM((n_heads * tq, lat), F32),
            pltpu.VMEM((n_heads * tq, LANES), F32),
            pltpu.VMEM((n_heads * tq, LANES), F32),
            pltpu.VMEM((n_heads * tq, LANES), F32),
            pltpu.VMEM((n_heads, 1, LANES), F32),
            pltpu.VMEM((tq, width), F32),
        ],
        compiler_params=_cparams(2),
        name="dsa_attention",
    )(qidx_arr, kidx, widx_arr, qa_arr, ckv, w_uk, w_uv, g.reshape(1, width))


def _sb_kernel(q_ref, k_ref, v_ref, o_ref, acc_ref, carry_ref, *, tq, hb):
    i = pl.program_id(2)
    t0 = i * tq
    row = lax.broadcasted_iota(jnp.int32, (tq, tq), 0)
    col = lax.broadcasted_iota(jnp.int32, (tq, tq), 1)
    suffix_ones = jnp.where(row > col, 1.0, 0.0).astype(BF16)
    acc_ref[...] = jnp.zeros(acc_ref.shape, F32)
    carry_ref[...] = jnp.zeros(carry_ref.shape, F32)

    def keep_going(state):
        j, carry_min = state
        return (j >= 0) & (carry_min <= F32_EXP_UNDERFLOW)

    def block(state):
        j, _ = state
        s0 = pl.multiple_of(j * tq, tq)
        strict = (s0 + col) < (t0 + row)
        carry_min = None
        for h in range(hb):
            cols = slice(h * HEAD_DIM, (h + 1) * HEAD_DIM)
            kb = k_ref[0, pl.ds(s0, tq), cols]
            vb = v_ref[0, pl.ds(s0, tq), cols]
            z = lax.dot_general(q_ref[0, :, cols], kb, NT_DIMS, preferred_element_type=F32) * ATTN_SCALE
            sp = jnp.maximum(z, 0.0) + jnp.log(1.0 + jnp.exp(-jnp.abs(z)))
            u = jnp.where(strict, sp, 0.0)
            u_hi, u_lo = _split_bf16(u)
            within = (jnp.dot(u_hi, suffix_ones, preferred_element_type=F32)
                      + jnp.dot(u_lo, suffix_ones, preferred_element_type=F32))
            carry = carry_ref[:, cols]
            a = jnp.where(strict, jnp.exp(z - sp - (_lane_tile(carry, tq) + within)), 0.0)
            acc_ref[:, cols] += jnp.dot(a.astype(BF16), vb, preferred_element_type=F32)
            carry = carry + jnp.sum(u, axis=1, keepdims=True)
            carry_ref[:, cols] = carry
            head_min = jnp.min(carry)
            carry_min = head_min if carry_min is None else jnp.minimum(carry_min, head_min)
        return j - 1, carry_min

    lax.while_loop(keep_going, block, (i, jnp.float32(0.0)))
    o_ref[0] = acc_ref[...]


def stick_breaking_attention(q, k, v, n_heads, tq=256, hb=4):
    (q_arr, q_blk), (k_arr, k_blk), (v_arr, v_blk) = q, k, v
    B, S, _ = q_arr.shape
    tq = min(tq, S)
    hb = min(hb, n_heads)
    wb = hb * HEAD_DIM
    return pl.pallas_call(
        functools.partial(_sb_kernel, tq=tq, hb=hb),
        grid=(B, n_heads // hb, S // tq),
        in_specs=[pl.BlockSpec((1, tq, wb), lambda b, h, i: (b, i, q_blk + h)),
                  pl.BlockSpec((1, S, wb), lambda b, h, i: (b, 0, k_blk + h)),
                  pl.BlockSpec((1, S, wb), lambda b, h, i: (b, 0, v_blk + h))],
        out_specs=pl.BlockSpec((1, tq, wb), lambda b, h, i: (b, i, h)),
        out_shape=jax.ShapeDtypeStruct((B, S, n_heads * HEAD_DIM), F32),
        scratch_shapes=[pltpu.VMEM((tq, wb), F32), pltpu.VMEM((tq, wb), F32)],
        compiler_params=_cparams(3),
        name="stick_breaking_attention",
    )(q_arr, k_arr, v_arr)


def _layer_norm(v, g, b):
    mu = jnp.mean(v, axis=-1, keepdims=True)
    d = v - mu
    var = jnp.mean(d * d, axis=-1, keepdims=True)
    return d * lax.rsqrt(var + EPS) * g + b


def _outproj_ln_kernel(od_ref, os_ref, gsb_ref, w_ref, x_ref, gate_ref, g_ref, b_ref, sc_ref, sh_ref,
                       x1_ref, h2_ref, a_ref, mix_ref, *, tn):
    j = pl.program_id(1)
    wd = od_ref.shape[1]

    @pl.when(j == 0)
    def _():
        a_ref[:, :wd] = od_ref[...]
        o = os_ref[...]
        ms = jnp.mean(o * o, axis=-1, keepdims=True)
        a_ref[:, wd:] = (o * lax.rsqrt(ms + EPS) * gsb_ref[...]).astype(BF16)

    mix_ref[:, pl.ds(pl.multiple_of(j * tn, tn), tn)] = jnp.dot(
        a_ref[...], w_ref[...], preferred_element_type=F32)

    @pl.when(j == pl.num_programs(1) - 1)
    def _():
        y = _layer_norm(DEEPNORM_ALPHA * x_ref[...] + gate_ref[0] * mix_ref[...], g_ref[...], b_ref[...])
        x1_ref[...] = y
        h2_ref[...] = (y * (1.0 + sc_ref[0]) + sh_ref[0]).astype(BF16)


def outproj_ln(o_dsa, o_sb, g_sb, w_out, x, gate, ln_g, ln_b, scale, shift, seq_len, tm=256, tn=512):
    M, D = x.shape
    wd, ws = o_dsa.shape[1], o_sb.shape[1]
    B = gate.shape[0]
    tm, tn = min(tm, seq_len), min(tn, D)

    def per_batch(i, j):
        return (i * tm // seq_len, 0, 0)

    row = lambda i, j: (i, 0)
    shared = lambda i, j: (0, 0)
    return pl.pallas_call(
        functools.partial(_outproj_ln_kernel, tn=tn),
        grid=(M // tm, D // tn),
        in_specs=[
            pl.BlockSpec((tm, wd), row),
            pl.BlockSpec((tm, ws), row),
            pl.BlockSpec((1, ws), shared),
            pl.BlockSpec((wd + ws, tn), lambda i, j: (0, j)),
            pl.BlockSpec((tm, D), row),
            pl.BlockSpec((1, 1, D), per_batch),
            pl.BlockSpec((1, D), shared),
            pl.BlockSpec((1, D), shared),
            pl.BlockSpec((1, 1, D), per_batch),
            pl.BlockSpec((1, 1, D), per_batch),
        ],
        out_specs=[pl.BlockSpec((tm, D), row), pl.BlockSpec((tm, D), row)],
        out_shape=[jax.ShapeDtypeStruct((M, D), F32), jax.ShapeDtypeStruct((M, D), BF16)],
        scratch_shapes=[pltpu.VMEM((tm, wd + ws), BF16), pltpu.VMEM((tm, D), F32)],
        compiler_params=_cparams(2),
        name="outproj_ln",
    )(o_dsa, o_sb, g_sb.reshape(1, ws), w_out, x, gate.reshape(B, 1, D), ln_g.reshape(1, D),
      ln_b.reshape(1, D), scale.reshape(B, 1, D), shift.reshape(B, 1, D))


def _scatter_rows_kernel(pos_ref, pad_end_ref, cnt_ref, h_ref, xg_ref, zero_ref, sem, *, tm, tt, top_k,
                         n_exp):
    i = pl.program_id(0)

    def fill_copy(e):
        start = pl.multiple_of(pad_end_ref[e] - tm, tm)
        return pltpu.make_async_copy(zero_ref, xg_ref.at[pl.ds(start, tm)], sem.at[0])

    @pl.when(i == 0)
    def _():
        zero_ref[...] = jnp.zeros(zero_ref.shape, zero_ref.dtype)

        def start_fill(e, c):
            @pl.when(cnt_ref[e] % tm != 0)
            def _():
                fill_copy(e).start()
            return c

        def wait_fill(e, c):
            @pl.when(cnt_ref[e] % tm != 0)
            def _():
                fill_copy(e).wait()
            return c

        def tail_copy(r):
            return pltpu.make_async_copy(
                zero_ref, xg_ref.at[pl.ds(pl.multiple_of(r * tm, tm), tm)], sem.at[0])

        def start_tail(r, c):
            tail_copy(r).start()
            return c

        def wait_tail(r, c):
            tail_copy(r).wait()
            return c

        n_used = pad_end_ref[n_exp - 1] // tm
        n_blocks = xg_ref.shape[0] // tm
        lax.fori_loop(0, n_exp, start_fill, 0)
        lax.fori_loop(n_used, n_blocks, start_tail, 0)
        lax.fori_loop(0, n_exp, wait_fill, 0)
        lax.fori_loop(n_used, n_blocks, wait_tail, 0)

    base = i * tt

    def row_copy(t, k):
        return pltpu.make_async_copy(h_ref.at[t], xg_ref.at[pos_ref[(base + t) * top_k + k]], sem.at[1])

    def start_rows(t, c):
        for k in range(top_k):
            row_copy(t, k).start(priority=k % 2)
        return c

    def wait_rows(t, c):
        for k in range(top_k):
            row_copy(t, k).wait()
        return c

    lax.fori_loop(0, tt, start_rows, 0)
    lax.fori_loop(0, tt, wait_rows, 0)


def scatter_rows(pos, pad_end, counts, h_slabs, n_rows, tm, tt=256):
    n_tok, P, _ = h_slabs.shape
    dtype = h_slabs.dtype
    tt = min(tt, n_tok)
    kernel = functools.partial(_scatter_rows_kernel, tm=tm, tt=tt, top_k=TOP_K_EXPERTS, n_exp=N_EXPERTS)
    return pl.pallas_call(
        kernel,
        grid_spec=pltpu.PrefetchScalarGridSpec(
            num_scalar_prefetch=3,
            grid=(n_tok // tt,),
            in_specs=[pl.BlockSpec((tt, P, LANES), lambda i, p, e, c: (i, 0, 0))],
            out_specs=pl.BlockSpec(memory_space=pl.ANY),
            scratch_shapes=[pltpu.VMEM((tm, P, LANES), dtype), pltpu.SemaphoreType.DMA((2,))],
        ),
        out_shape=jax.ShapeDtypeStruct((n_rows, P, LANES), dtype),
        compiler_params=_cparams(1),
        name="scatter_rows",
    )(pos, pad_end, counts, h_slabs)


def _expert_changed(blk_e_ref, r):
    return (r == 0) | (blk_e_ref[r] != blk_e_ref[jnp.maximum(r - 1, 0)])


def _moe_up_kernel(blk_e_ref, n_used_ref, x_ref, wg_ref, wl_ref, bg_ref, bl_ref, o_ref, wg_bf, wl_bf):
    r = pl.program_id(1)
    used = r < n_used_ref[0]

    @pl.when(used & _expert_changed(blk_e_ref, r))
    def _():
        wg_bf[...] = wg_ref[0].astype(BF16)
        wl_bf[...] = wl_ref[0].astype(BF16)

    @pl.when(used)
    def _():
        x = x_ref[...]
        glu = jnp.dot(x, wg_bf[...], preferred_element_type=F32) + bg_ref[0]
        lin = jnp.dot(x, wl_bf[...], preferred_element_type=F32) + bl_ref[0]
        glu = jnp.minimum(glu, SWIGLU_LIMIT)
        lin = jnp.clip(lin, -SWIGLU_LIMIT, SWIGLU_LIMIT)
        act = glu * (1.0 / (1.0 + jnp.exp(-SWIGLU_ALPHA * glu))) * (lin + 1.0)
        o_ref[...] = act.astype(o_ref.dtype)

    @pl.when(jnp.logical_not(used))
    def _():
        o_ref[...] = jnp.zeros(o_ref.shape, o_ref.dtype)


def _moe_down_kernel(blk_e_ref, n_used_ref, a_ref, wd_ref, bd_ref, o_ref, wd_bf):
    r = pl.program_id(1)
    used = r < n_used_ref[0]

    @pl.when(used & _expert_changed(blk_e_ref, r))
    def _():
        wd_bf[...] = wd_ref[0].astype(BF16)

    @pl.when(used)
    def _():
        out = jnp.dot(a_ref[...], wd_bf[...], preferred_element_type=F32) + bd_ref[0]
        for s in range(o_ref.shape[1]):
            o_ref[:, s, :] = out[:, s * LANES:(s + 1) * LANES]

    @pl.when(jnp.logical_not(used))
    def _():
        o_ref[...] = jnp.zeros(o_ref.shape, o_ref.dtype)


def moe_experts(xg, blk_e, n_used, w_up, b_up, w_down, b_down, tm, tn_up=512, tn_down=1024):
    R, D = xg.shape
    n_exp, d_exp = w_down.shape[0], w_down.shape[1]
    tn_up, tn_down = min(tn_up, d_exp), min(tn_down, D)
    n_up = d_exp // tn_up

    def row(r, n):
        return jnp.minimum(r, n[0] - 1)

    def expert(r, e, n):
        return e[row(r, n)]

    b_up3 = b_up.reshape(n_exp, 1, 2 * d_exp)
    act = pl.pallas_call(
        _moe_up_kernel,
        grid_spec=pltpu.PrefetchScalarGridSpec(
            num_scalar_prefetch=2,
            grid=(n_up, R // tm),
            in_specs=[
                pl.BlockSpec((tm, D), lambda c, r, e, n: (row(r, n), 0)),
                pl.BlockSpec((1, D, tn_up), lambda c, r, e, n: (expert(r, e, n), 0, c)),
                pl.BlockSpec((1, D, tn_up), lambda c, r, e, n: (expert(r, e, n), 0, n_up + c)),
                pl.BlockSpec((1, 1, tn_up), lambda c, r, e, n: (expert(r, e, n), 0, c)),
                pl.BlockSpec((1, 1, tn_up), lambda c, r, e, n: (expert(r, e, n), 0, n_up + c)),
            ],
            out_specs=pl.BlockSpec((tm, tn_up), lambda c, r, e, n: (r, c)),
            scratch_shapes=[pltpu.VMEM((D, tn_up), BF16), pltpu.VMEM((D, tn_up), BF16)],
        ),
        out_shape=jax.ShapeDtypeStruct((R, d_exp), BF16),
        compiler_params=_cparams(2),
        name="moe_up",
    )(blk_e, n_used, xg, w_up, w_up, b_up3, b_up3)
    return pl.pallas_call(
        _moe_down_kernel,
        grid_spec=pltpu.PrefetchScalarGridSpec(
            num_scalar_prefetch=2,
            grid=(D // tn_down, R // tm),
            in_specs=[
                pl.BlockSpec((tm, d_exp), lambda c, r, e, n: (row(r, n), 0)),
                pl.BlockSpec((1, d_exp, tn_down), lambda c, r, e, n: (expert(r, e, n), 0, c)),
                pl.BlockSpec((1, 1, tn_down), lambda c, r, e, n: (expert(r, e, n), 0, c)),
            ],
            out_specs=pl.BlockSpec((tm, tn_down // LANES, LANES), lambda c, r, e, n: (r, c, 0)),
            scratch_shapes=[pltpu.VMEM((d_exp, tn_down), BF16)],
        ),
        out_shape=jax.ShapeDtypeStruct((R, D // LANES, LANES), F32),
        compiler_params=_cparams(2),
        name="moe_down",
    )(blk_e, n_used, act, w_down, b_down.reshape(n_exp, 1, D))


def _combine_ln_kernel(pos_ref, gates_ref, rows_ref, x_ref, gate_ref, g_ref, b_ref, o_ref,
                       buf0, buf1, ysl_ref, y_ref, sem, *, tt, top_k):
    i = pl.program_id(0)
    n_blocks = pl.num_programs(0)

    def row_copy(blk, buf, slot, t, k):
        src = rows_ref.at[pos_ref[(blk * tt + t) * top_k + k]]
        return pltpu.make_async_copy(src, buf.at[k * tt + t], sem.at[slot])

    def start_block(blk, buf, slot):
        def body(t, c):
            for k in range(top_k):
                row_copy(blk, buf, slot, t, k).start(priority=k % 2)
            return c
        lax.fori_loop(0, tt, body, 0)

    def wait_block(blk, buf, slot):
        def body(t, c):
            for k in range(top_k):
                row_copy(blk, buf, slot, t, k).wait()
            return c
        lax.fori_loop(0, tt, body, 0)

    def combine(buf):
        def per_token(t, c):
            base = (i * tt + t) * top_k
            acc = gates_ref[base] * buf[t]
            for k in range(1, top_k):
                acc = acc + gates_ref[base + k] * buf[k * tt + t]
            ysl_ref[t] = acc
            return c
        lax.fori_loop(0, tt, per_token, 0, unroll=4)
        for s in range(buf.shape[1]):
            y_ref[:, s * LANES:(s + 1) * LANES] = ysl_ref[:, s, :]

    even = i % 2 == 0

    @pl.when(i == 0)
    def _():
        start_block(0, buf0, 0)

    @pl.when((i + 1 < n_blocks) & even)
    def _():
        start_block(i + 1, buf1, 1)

    @pl.when((i + 1 < n_blocks) & jnp.logical_not(even))
    def _():
        start_block(i + 1, buf0, 0)

    @pl.when(even)
    def _():
        wait_block(i, buf0, 0)
        combine(buf0)

    @pl.when(jnp.logical_not(even))
    def _():
        wait_block(i, buf1, 1)
        combine(buf1)

    o_ref[...] = _layer_norm(DEEPNORM_ALPHA * x_ref[...] + gate_ref[0] * y_ref[...], g_ref[...], b_ref[...])


def combine_ln(pos, gates, out_rows, x, gate, ln_g, ln_b, seq_len, tt=128):
    n_tok, D = x.shape
    B = gate.shape[0]
    tt = min(tt, seq_len)
    P = out_rows.shape[1]
    kernel = functools.partial(_combine_ln_kernel, tt=tt, top_k=TOP_K_EXPERTS)
    return pl.pallas_call(
        kernel,
        grid_spec=pltpu.PrefetchScalarGridSpec(
            num_scalar_prefetch=2,
            grid=(n_tok // tt,),
            in_specs=[
                pl.BlockSpec(memory_space=pl.ANY),
                pl.BlockSpec((tt, D), lambda i, p, q: (i, 0)),
                pl.BlockSpec((1, 1, D), lambda i, p, q: (i * tt // seq_len, 0, 0)),
                pl.BlockSpec((1, D), lambda i, p, q: (0, 0)),
                pl.BlockSpec((1, D), lambda i, p, q: (0, 0)),
            ],
            out_specs=pl.BlockSpec((tt, D), lambda i, p, q: (i, 0)),
            scratch_shapes=[pltpu.VMEM((TOP_K_EXPERTS * tt, P, LANES), F32),
                            pltpu.VMEM((TOP_K_EXPERTS * tt, P, LANES), F32),
                            pltpu.VMEM((tt, P, LANES), F32),
                            pltpu.VMEM((tt, D), F32),
                            pltpu.SemaphoreType.DMA((2,))],
        ),
        out_shape=jax.ShapeDtypeStruct((n_tok, D), F32),
        compiler_params=_cparams(1),
        name="combine_ln",
    )(pos, gates, out_rows, x, gate.reshape(B, 1, D), ln_g.reshape(1, D), ln_b.reshape(1, D))


def route(logits, tm):
    n_tok = logits.shape[0]
    top_logit, top_e = lax.top_k(logits, TOP_K_EXPERTS)
    gates = jax.nn.softmax(top_logit, axis=-1)
    n_asg = n_tok * TOP_K_EXPERTS
    e_flat = top_e.reshape(n_asg)
    onehot = (e_flat[:, None] == jnp.arange(N_EXPERTS, dtype=e_flat.dtype)[None, :]).astype(jnp.int32)
    running = jnp.cumsum(onehot, axis=0)
    rank = jnp.take_along_axis(running, e_flat[:, None], axis=1)[:, 0] - 1
    counts = running[-1]
    padded = (counts + tm - 1) // tm * tm
    pad_end = jnp.cumsum(padded)
    pad_start = pad_end - padded
    pos = (pad_start[e_flat] + rank).astype(jnp.int32)
    n_blocks = -(-n_asg // tm) + N_EXPERTS
    blk_e = jnp.minimum(jnp.searchsorted(pad_end, jnp.arange(n_blocks) * tm, side='right'),
                        N_EXPERTS - 1).astype(jnp.int32)
    n_used = (pad_end[-1] // tm).astype(jnp.int32).reshape(1)
    return (gates, pos, blk_e, n_used, pad_end.astype(jnp.int32), counts.astype(jnp.int32),
            n_blocks * tm)


def router_logits(h2, w_router, b_router):
    D = h2.shape[1]
    wr = jnp.zeros((D, LANES), BF16).at[:, :N_EXPERTS].set(w_router.astype(BF16))
    return matmul(h2, wr, F32, tm=512, tn=LANES)[:, :N_EXPERTS] + b_router.astype(F32)


def moe_block(h2, x1, gate_f, ln_g, ln_b, w_router, b_router, w_up, b_up, w_down, b_down, seq_len, tm=512):
    n_tok, D = h2.shape
    logits = router_logits(h2, w_router, b_router)
    gates, pos, blk_e, n_used, pad_end, counts, n_rows = route(logits, tm)
    xg = scatter_rows(pos, pad_end, counts, h2.reshape(n_tok, D // LANES, LANES), n_rows, tm)
    out_rows = moe_experts(xg.reshape(n_rows, D), blk_e, n_used, w_up, b_up, w_down, b_down, tm)
    return combine_ln(pos, gates.reshape(-1), out_rows, x1, gate_f, ln_g, ln_b, seq_len)


def _layer(x, c, w_ada, b_ada, w_in, kv_norm_g, w_uk, w_uv, grp_norm_dsa, grp_norm_sb, w_out,
           ln1_g, ln1_b, w_router, b_router, w_up, b_up, w_down, b_down, ln2_g, ln2_b):
    B, S, D = x.shape
    n_tok = B * S
    n_dsa = w_uk.shape[0]
    w_qa = n_dsa * HEAD_DIM
    w_qidx = IDX_HEADS * IDX_DIM
    w_sb = grp_norm_sb.shape[0]
    n_sb = w_sb // HEAD_DIM

    mod = ada_modulation(c, w_ada, b_ada)
    shift_a, scale_a, gate_a, shift_f, scale_f, gate_f = jnp.split(mod, 6, axis=-1)

    o1 = w_qa
    o2 = o1 + KV_LATENT
    o3 = o2 + w_qidx
    o4 = o3 + IDX_DIM
    o5 = o4 + IDX_HEADS
    w_main = jnp.concatenate([w_in[:, o2:o3], w_in[:, :o1], w_in[:, o5:]], axis=1).astype(BF16)
    n_small = KV_LATENT + IDX_DIM + IDX_HEADS
    n_small_pad = KV_LATENT + IDX_DIM + LANES
    w_small = jnp.concatenate(
        [w_in[:, o1:o2], w_in[:, o3:o5], jnp.zeros((D, n_small_pad - n_small), F32)], axis=1).astype(BF16)

    h = modulate(x, scale_a, shift_a).reshape(n_tok, D)
    proj = matmul(h, w_main, BF16, tm=512, tn=512).reshape(B, S, w_main.shape[1])
    small = matmul(h, w_small, F32, tm=512, tn=n_small_pad)
    ckv, kidx = kv_prep(small, kv_norm_g)

    sb_w = 4 * HEAD_DIM
    o_dsa = dsa_attention(
        qidx=(proj, 0), kidx=kidx.reshape(B, S, IDX_DIM),
        widx=(small.reshape(B, S, n_small_pad), (KV_LATENT + IDX_DIM) // LANES),
        qa=(proj, w_qidx // w_qa), ckv=ckv.reshape(B, S, KV_LATENT),
        w_uk=w_uk.astype(BF16), w_uv=w_uv.astype(BF16), g=grp_norm_dsa)
    sb0 = w_qidx + w_qa
    o_sb = stick_breaking_attention((proj, sb0 // sb_w), (proj, (sb0 + w_sb) // sb_w),
                                    (proj, (sb0 + 2 * w_sb) // sb_w), n_sb)

    x1, h2 = outproj_ln(o_dsa.reshape(n_tok, w_qa), o_sb.reshape(n_tok, w_sb), grp_norm_sb,
                        w_out.astype(BF16), x.reshape(n_tok, D), gate_a, ln1_g, ln1_b, scale_f, shift_f, S)
    out = moe_block(h2, x1, gate_f, ln2_g, ln2_b, w_router, b_router, w_up, b_up, w_down, b_down, S)
    return out.reshape(B, S, D)


def kernel(x, c, w_ada, b_ada, w_in, kv_norm_g, w_uk, w_uv, grp_norm_dsa, grp_norm_sb, w_out, ln1_g, ln1_b, w_router, b_router, w_up, b_up, w_down, b_down, ln2_g, ln2_b):
    return _layer(x, c, w_ada[0], b_ada[0], w_in[0], kv_norm_g[0], w_uk[0], w_uv[0],
                  grp_norm_dsa[0], grp_norm_sb[0], w_out[0], ln1_g[0], ln1_b[0], w_router[0],
                  b_router[0], w_up[0], b_up[0], w_down[0], b_down[0], ln2_g[0], ln2_b[0])
```

```python
import functools
import math

import jax
import jax.numpy as jnp
from jax import lax
from jax.experimental import pallas as pl
from jax.experimental.pallas import tpu as pltpu

F32 = jnp.float32
BF16 = jnp.bfloat16

HEAD_DIM = 128
KV_LATENT = 512
IDX_HEADS = 32
IDX_DIM = 128
TOPK_MAX = 256
N_EXPERTS = 32
TOP_K_EXPERTS = 4
SWIGLU_ALPHA = 1.702
SWIGLU_LIMIT = 7.0
DEPTH = 1
DEEPNORM_ALPHA = (2 * DEPTH) ** 0.25
EPS = 1e-5
INDEX_SCALE = (IDX_HEADS * IDX_DIM) ** -0.5
ATTN_SCALE = HEAD_DIM ** -0.5
LOG2_E = math.log2(math.e)

LANES = 128
VMEM_LIMIT_BYTES = 56 * 1024 * 1024
VMEM_LIMIT_WIDE_ROWS_BYTES = 60 * 1024 * 1024

MASKED_LOGIT = -1e30
F32_EXP_UNDERFLOW = 104.0

NT_DIMS = (((1,), (1,)), ((), ()))


def _cparams(n_axes):
    return pltpu.CompilerParams(
        dimension_semantics=("arbitrary",) * n_axes, vmem_limit_bytes=VMEM_LIMIT_BYTES)


def _split_bf16(v):
    hi = v.astype(BF16)
    lo = (v - hi.astype(F32)).astype(BF16)
    return hi, lo


def _lane_tile(v, width):
    return jnp.concatenate([v] * (width // LANES), axis=1)


def _ada_kernel(c_ref, w_ref, b_ref, o_ref):
    c = c_ref[...]
    s = c * (1.0 / (1.0 + jnp.exp(-c)))
    s_hi, s_lo = _split_bf16(s)
    w_hi, w_lo = _split_bf16(w_ref[...])
    acc = jnp.dot(s_hi, w_hi, preferred_element_type=F32)
    acc += jnp.dot(s_hi, w_lo, preferred_element_type=F32)
    acc += jnp.dot(s_lo, w_hi, preferred_element_type=F32)
    o_ref[...] = acc + b_ref[...]


def ada_modulation(c, w_ada, b_ada, tn=512):
    B, D = c.shape
    N = w_ada.shape[1]
    rows = 8
    c_pad = jnp.zeros((rows, D), F32).at[:B].set(c)
    out = pl.pallas_call(
        _ada_kernel,
        grid=(N // tn,),
        in_specs=[pl.BlockSpec((rows, D), lambda j: (0, 0)),
                  pl.BlockSpec((D, tn), lambda j: (0, j)),
                  pl.BlockSpec((1, tn), lambda j: (0, j))],
        out_specs=pl.BlockSpec((rows, tn), lambda j: (0, j)),
        out_shape=jax.ShapeDtypeStruct((rows, N), F32),
        compiler_params=_cparams(1),
        name="ada_modulation",
    )(c_pad, w_ada, b_ada.reshape(1, N))
    return out[:B]


def _modulate_kernel(x_ref, sc_ref, sh_ref, o_ref):
    o_ref[0] = (x_ref[0] * (1.0 + sc_ref[0]) + sh_ref[0]).astype(o_ref.dtype)


def modulate(x, scale, shift, tm=512):
    B, S, D = x.shape
    tm = min(tm, S)
    vec = pl.BlockSpec((1, 1, D), lambda b, i: (b, 0, 0))
    return pl.pallas_call(
        _modulate_kernel,
        grid=(B, S // tm),
        in_specs=[pl.BlockSpec((1, tm, D), lambda b, i: (b, i, 0)), vec, vec],
        out_specs=pl.BlockSpec((1, tm, D), lambda b, i: (b, i, 0)),
        out_shape=jax.ShapeDtypeStruct((B, S, D), BF16),
        compiler_params=_cparams(2),
        name="modulate",
    )(x, scale.reshape(B, 1, D), shift.reshape(B, 1, D))


def _matmul_kernel(a_ref, b_ref, o_ref):
    o_ref[...] = jnp.dot(a_ref[...], b_ref[...], preferred_element_type=F32).astype(o_ref.dtype)


def matmul(a, b, out_dtype, tm=512, tn=512):
    M, K = a.shape
    N = b.shape[1]
    tm, tn = min(tm, M), min(tn, N)
    return pl.pallas_call(
        _matmul_kernel,
        grid=(M // tm, N // tn),
        in_specs=[pl.BlockSpec((tm, K), lambda i, j: (i, 0)),
                  pl.BlockSpec((K, tn), lambda i, j: (0, j))],
        out_specs=pl.BlockSpec((tm, tn), lambda i, j: (i, j)),
        out_shape=jax.ShapeDtypeStruct((M, N), out_dtype),
        compiler_params=_cparams(2),
        name="matmul",
    )(a, b)


def _kv_prep_kernel(s_ref, g_ref, ckv_ref, kidx_ref):
    x = s_ref[:, :KV_LATENT]
    ms = jnp.mean(x * x, axis=-1, keepdims=True)
    ckv_ref[...] = (x * lax.rsqrt(ms + EPS) * g_ref[...]).astype(BF16)
    kidx_ref[...] = s_ref[:, KV_LATENT:KV_LATENT + IDX_DIM].astype(BF16)


def kv_prep(small, g, tm=512):
    M, W = small.shape
    tm = min(tm, M)
    return pl.pallas_call(
        _kv_prep_kernel,
        grid=(M // tm,),
        in_specs=[pl.BlockSpec((tm, W), lambda i: (i, 0)), pl.BlockSpec((1, KV_LATENT), lambda i: (0, 0))],
        out_specs=[pl.BlockSpec((tm, KV_LATENT), lambda i: (i, 0)), pl.BlockSpec((tm, IDX_DIM), lambda i: (i, 0))],
        out_shape=[jax.ShapeDtypeStruct((M, KV_LATENT), BF16), jax.ShapeDtypeStruct((M, IDX_DIM), BF16)],
        compiler_params=_cparams(1),
        name="kv_prep",
    )(small, g.reshape(1, KV_LATENT))


def _sortable_key(v):
    bits = pltpu.bitcast(v + 0.0, jnp.int32)
    return bits ^ (lax.shift_right_arithmetic(bits, 31) & jnp.int32(0x7FFFFFFF))


def _dsa_kernel(qidx_ref, kidx_ref, widx_ref, qa_ref, ckv_ref, wuk_ref, wuv_ref, g_ref, o_ref,
                key_ref, wb_ref, q2_ref, qlat_ref, lg_ref, bias_ref, p_ref, acc_ref, m_ref, l_ref,
                alpha_ref, slope_ref, tie_end_ref, obuf_ref, *, tq, tk_score, tk, k_sel, n_heads, n_idx):
    s_len = key_ref.shape[1]
    pos_bits = max(1, (s_len - 1).bit_length())
    i = pl.program_id(1)
    t0 = i * tq
    n_att = (t0 + tq - 1) // tk + 1
    n_score = n_att * (tk // tk_score)

    w = widx_ref[0]
    for h in range(n_idx):
        wb_ref[h] = jnp.broadcast_to(w[:, h:h + 1], (tq, LANES))
        q2_ref[h * tq:(h + 1) * tq, :] = qidx_ref[0, :, h * IDX_DIM:(h + 1) * IDX_DIM]
    t_ids_s = t0 + lax.broadcasted_iota(jnp.int32, (tq, tk_score), 0)
    lane_ids_s = lax.broadcasted_iota(jnp.int32, (tq, tk_score), 1)

    def score_chunk(j, carry):
        s0 = pl.multiple_of(j * tk_score, tk_score)
        kb = kidx_ref[0, pl.ds(s0, tk_score), :]
        lg = lax.dot_general(q2_ref[...], kb, NT_DIMS, preferred_element_type=F32)
        r = jnp.maximum(lg, 0.0).reshape(n_idx, tq, tk_score)
        parts = [jnp.sum(r[:, :, c * LANES:(c + 1) * LANES] * wb_ref[...], axis=0)
                 for c in range(tk_score // LANES)]
        sc = jnp.concatenate(parts, axis=1) * INDEX_SCALE
        sc = jnp.where(s0 + lane_ids_s <= t_ids_s, sc, -jnp.inf)
        key_ref[:, pl.ds(s0, tk_score)] = _sortable_key(sc)
        return carry

    lax.fori_loop(0, n_score, score_chunk, 0)

    def bisect(it, prefix):
        cand = prefix + lax.shift_left(jnp.int32(1), 31 - it)

        def count_chunk(j, cnt):
            s0 = pl.multiple_of(j * tk, tk)
            keys = key_ref[:, pl.ds(s0, tk)]
            for c in range(tk // LANES):
                cnt = cnt + jnp.where(keys[:, c * LANES:(c + 1) * LANES] >= cand, 1.0, 0.0)
            return cnt

        cnt = lax.fori_loop(0, n_att, count_chunk, jnp.zeros((tq, LANES), F32))
        total = jnp.sum(cnt, axis=1, keepdims=True)
        return jnp.where(total >= float(k_sel), cand, prefix)

    thr = lax.fori_loop(0, 32, bisect, jnp.full((tq, LANES), -2 ** 31, jnp.int32))

    def count_where(pred):
        def chunk(j, cnt):
            s0 = pl.multiple_of(j * tk, tk)
            keys = key_ref[:, pl.ds(s0, tk)]
            for c in range(tk // LANES):
                s_ids = s0 + c * LANES + lax.broadcasted_iota(jnp.int32, (tq, LANES), 1)
                cnt = cnt + jnp.where(pred(keys[:, c * LANES:(c + 1) * LANES], s_ids), 1.0, 0.0)
            return cnt
        cnt = lax.fori_loop(0, n_att, chunk, jnp.zeros((tq, LANES), F32))
        return jnp.sum(cnt, axis=1, keepdims=True)

    n_above = count_where(lambda keys, s_ids: keys > thr)
    n_at_least = count_where(lambda keys, s_ids: keys >= thr)
    need = float(k_sel) - n_above
    tie_end_ref[...] = jnp.full((tq, LANES), s_len, jnp.int32)

    @pl.when(jnp.max(n_at_least) > float(k_sel))
    def _():
        def grow(it, p):
            cand = p | lax.shift_left(jnp.int32(1), pos_bits - 1 - it)
            below = count_where(lambda keys, s_ids: (keys == thr) & (s_ids < cand))
            return jnp.where(below < need, cand, p)
        tie_end_ref[...] = lax.fori_loop(0, pos_bits, grow, jnp.zeros((tq, LANES), jnp.int32))

    for h in range(n_heads):
        ql = jnp.dot(qa_ref[0, :, h * HEAD_DIM:(h + 1) * HEAD_DIM], wuk_ref[h], preferred_element_type=F32)
        qlat_ref[h * tq:(h + 1) * tq, :] = (ql * (ATTN_SCALE * LOG2_E)).astype(BF16)
    head_no = lax.broadcasted_iota(jnp.int32, slope_ref.shape, 0).astype(F32)
    slope_ref[...] = jnp.exp2(-8.0 * (head_no + 1.0) / n_heads) * LOG2_E
    m_ref[...] = jnp.full(m_ref.shape, MASKED_LOGIT, F32)
    l_ref[...] = jnp.zeros(l_ref.shape, F32)
    acc_ref[...] = jnp.zeros(acc_ref.shape, F32)
    t_ids = t0 + lax.broadcasted_iota(jnp.int32, (tq, tk), 0)
    lane_ids = lax.broadcasted_iota(jnp.int32, (tq, tk), 1)
    col_ids = lax.broadcasted_iota(jnp.int32, (1, tk), 1)

    def attend_chunk(j, carry):
        s0 = pl.multiple_of(j * tk, tk)
        cb = ckv_ref[0, pl.ds(s0, tk), :]
        lg_ref[...] = lax.dot_general(qlat_ref[...], cb, NT_DIMS, preferred_element_type=F32)
        keys = key_ref[:, pl.ds(s0, tk)]
        s_ids = s0 + lane_ids
        thr_t = _lane_tile(thr, tk)
        sel = (keys > thr_t) | ((keys == thr_t) & (s_ids <= _lane_tile(tie_end_ref[...], tk)))
        sel = sel & (s_ids <= t_ids)
        bias_ref[...] = jnp.where(sel, 0.0, MASKED_LOGIT)
        rel = (s0 - t0 + col_ids).astype(F32)

        def head(h, c2):
            rows = pl.ds(pl.multiple_of(h * tq, tq), tq)
            x = lg_ref[rows, :] + _lane_tile(slope_ref[h], tk) * rel + bias_ref[...]
            m_old = m_ref[rows, :]
            m_new = jnp.maximum(m_old, jnp.max(x, axis=1, keepdims=True))
            alpha = jnp.exp2(m_old - m_new)
            p = jnp.exp2(x - _lane_tile(m_new, tk))
            l_ref[rows, :] = alpha * l_ref[rows, :] + jnp.sum(p, axis=1, keepdims=True)
            m_ref[rows, :] = m_new
            alpha_ref[rows, :] = alpha
            p_ref[rows, :] = p.astype(BF16)
            return c2

        lax.fori_loop(0, n_heads, head, 0, unroll=True)
        lat = acc_ref.shape[1]
        acc_ref[...] = (acc_ref[...] * _lane_tile(alpha_ref[...], lat)
                        + jnp.dot(p_ref[...], cb, preferred_element_type=F32))
        return carry

    lax.fori_loop(0, n_att, attend_chunk, 0)

    ss = jnp.zeros((tq, 1), F32)
    for h in range(n_heads):
        rows = slice(h * tq, (h + 1) * tq)
        inv_l = _lane_tile(1.0 / l_ref[rows, :], acc_ref.shape[1])
        oh = jnp.dot((acc_ref[rows, :] * inv_l).astype(BF16), wuv_ref[h],
                     preferred_element_type=F32)
        obuf_ref[:, h * HEAD_DIM:(h + 1) * HEAD_DIM] = oh
        ss = ss + jnp.sum(oh * oh, axis=1, keepdims=True)
    inv = lax.rsqrt(ss * (1.0 / (n_heads * HEAD_DIM)) + EPS)
    o_ref[0] = (obuf_ref[...] * inv * g_ref[...]).astype(o_ref.dtype)


def dsa_attention(qidx, kidx, widx, qa, ckv, w_uk, w_uv, g, tq=128, tk_score=512, tk=512):
    qidx_arr, qidx_blk = qidx
    qa_arr, qa_blk = qa
    widx_arr, widx_blk = widx
    B, S, lat = ckv.shape
    n_heads = w_uk.shape[0]
    n_idx = IDX_HEADS
    tq, tk = min(tq, S), min(tk, S)
    tk_score = min(tk_score, tk)
    k_sel = min(TOPK_MAX, S // 4)
    width = n_heads * HEAD_DIM
    kernel = functools.partial(_dsa_kernel, tq=tq, tk_score=tk_score, tk=tk, k_sel=k_sel,
                               n_heads=n_heads, n_idx=n_idx)
    once = pl.Buffered(1)
    return pl.pallas_call(
        kernel,
        grid=(B, S // tq),
        in_specs=[
            pl.BlockSpec((1, tq, n_idx * IDX_DIM), lambda b, i: (b, i, qidx_blk)),
            pl.BlockSpec((1, S, IDX_DIM), lambda b, i: (b, 0, 0), pipeline_mode=once),
            pl.BlockSpec((1, tq, LANES), lambda b, i: (b, i, widx_blk)),
            pl.BlockSpec((1, tq, width), lambda b, i: (b, i, qa_blk)),
            pl.BlockSpec((1, S, lat), lambda b, i: (b, 0, 0), pipeline_mode=once),
            pl.BlockSpec((n_heads, HEAD_DIM, lat), lambda b, i: (0, 0, 0), pipeline_mode=once),
            pl.BlockSpec((n_heads, lat, HEAD_DIM), lambda b, i: (0, 0, 0), pipeline_mode=once),
            pl.BlockSpec((1, width), lambda b, i: (0, 0)),
        ],
        out_specs=pl.BlockSpec((1, tq, width), lambda b, i: (b, i, 0)),
        out_shape=jax.ShapeDtypeStruct((B, S, width), BF16),
        scratch_shapes=[
            pltpu.VMEM((tq, S), jnp.int32),
            pltpu.VMEM((n_idx, tq, LANES), F32),
            pltpu.VMEM((n_idx * tq, IDX_DIM), BF16),
            pltpu.VMEM((n_heads * tq, lat), BF16),
            pltpu.VMEM((n_heads * tq, tk), F32),
            pltpu.VMEM((tq, tk), F32),
            pltpu.VMEM((n_heads * tq, tk), BF16),
            pltpu.VMEM((n_heads * tq, lat), F32),
            pltpu.VMEM((n_heads * tq, LANES), F32),
            pltpu.VMEM((n_heads * tq, LANES), F32),
            pltpu.VMEM((n_heads * tq, LANES), F32),
            pltpu.VMEM((n_heads, 1, LANES), F32),
            pltpu.VMEM((tq, LANES), jnp.int32),
            pltpu.VMEM((tq, width), F32),
        ],
        compiler_params=_cparams(2),
        name="dsa_attention",
    )(qidx_arr, kidx, widx_arr, qa_arr, ckv, w_uk, w_uv, g.reshape(1, width))


def _sb_kernel(q_ref, k_ref, v_ref, o_ref, acc_ref, carry_ref, *, tq, hb):
    i = pl.program_id(2)
    t0 = i * tq
    row = lax.broadcasted_iota(jnp.int32, (tq, tq), 0)
    col = lax.broadcasted_iota(jnp.int32, (tq, tq), 1)
    suffix_ones = jnp.where(row > col, 1.0, 0.0).astype(BF16)
    acc_ref[...] = jnp.zeros(acc_ref.shape, F32)
    carry_ref[...] = jnp.zeros(carry_ref.shape, F32)

    def keep_going(state):
        j, carry_min = state
        return (j >= 0) & (carry_min <= F32_EXP_UNDERFLOW)

    def block(state):
        j, _ = state
        s0 = pl.multiple_of(j * tq, tq)
        strict = (s0 + col) < (t0 + row)
        carry_min = None
        for h in range(hb):
            cols = slice(h * HEAD_DIM, (h + 1) * HEAD_DIM)
            kb = k_ref[0, pl.ds(s0, tq), cols]
            vb = v_ref[0, pl.ds(s0, tq), cols]
            z = lax.dot_general(q_ref[0, :, cols], kb, NT_DIMS, preferred_element_type=F32) * ATTN_SCALE
            sp = jnp.maximum(z, 0.0) + jnp.log(1.0 + jnp.exp(-jnp.abs(z)))
            u = jnp.where(strict, sp, 0.0)
            u_hi, u_lo = _split_bf16(u)
            within = (jnp.dot(u_hi, suffix_ones, preferred_element_type=F32)
                      + jnp.dot(u_lo, suffix_ones, preferred_element_type=F32))
            carry = carry_ref[:, cols]
            a = jnp.where(strict, jnp.exp(z - sp - (_lane_tile(carry, tq) + within)), 0.0)
            acc_ref[:, cols] += jnp.dot(a.astype(BF16), vb, preferred_element_type=F32)
            carry = carry + jnp.sum(u, axis=1, keepdims=True)
            carry_ref[:, cols] = carry
            head_min = jnp.min(carry)
            carry_min = head_min if carry_min is None else jnp.minimum(carry_min, head_min)
        return j - 1, carry_min

    lax.while_loop(keep_going, block, (i, jnp.float32(0.0)))
    o_ref[0] = acc_ref[...]


SB_HEADS_PER_STEP = 8


def stick_breaking_attention(q, k, v, n_heads, tq=256, hb=SB_HEADS_PER_STEP):
    (q_arr, q_blk), (k_arr, k_blk), (v_arr, v_blk) = q, k, v
    B, S, _ = q_arr.shape
    tq = min(tq, S)
    wb = hb * HEAD_DIM
    once = pl.Buffered(1)
    return pl.pallas_call(
        functools.partial(_sb_kernel, tq=tq, hb=hb),
        grid=(B, n_heads // hb, S // tq),
        in_specs=[pl.BlockSpec((1, tq, wb), lambda b, h, i: (b, i, q_blk + h)),
                  pl.BlockSpec((1, S, wb), lambda b, h, i: (b, 0, k_blk + h), pipeline_mode=once),
                  pl.BlockSpec((1, S, wb), lambda b, h, i: (b, 0, v_blk + h), pipeline_mode=once)],
        out_specs=pl.BlockSpec((1, tq, wb), lambda b, h, i: (b, i, h)),
        out_shape=jax.ShapeDtypeStruct((B, S, n_heads * HEAD_DIM), F32),
        scratch_shapes=[pltpu.VMEM((tq, wb), F32), pltpu.VMEM((tq, wb), F32)],
        compiler_params=_cparams(3),
        name="stick_breaking_attention",
    )(q_arr, k_arr, v_arr)


def _layer_norm(v, g, b):
    mu = jnp.mean(v, axis=-1, keepdims=True)
    d = v - mu
    var = jnp.mean(d * d, axis=-1, keepdims=True)
    return d * lax.rsqrt(var + EPS) * g + b


def _outproj_ln_kernel(od_ref, os_ref, gsb_ref, w_ref, x_ref, gate_ref, g_ref, b_ref, sc_ref, sh_ref,
                       x1_ref, h2_ref, a_ref, *, tn):
    j = pl.program_id(1)
    wd = od_ref.shape[1]

    @pl.when(j == 0)
    def _():
        a_ref[:, :wd] = od_ref[...]
        o = os_ref[...]
        ms = jnp.mean(o * o, axis=-1, keepdims=True)
        a_ref[:, wd:] = (o * lax.rsqrt(ms + EPS) * gsb_ref[...]).astype(BF16)

    x1_ref[:, pl.ds(pl.multiple_of(j * tn, tn), tn)] = jnp.dot(
        a_ref[...], w_ref[...], preferred_element_type=F32)

    @pl.when(j == pl.num_programs(1) - 1)
    def _():
        y = _layer_norm(DEEPNORM_ALPHA * x_ref[...] + gate_ref[0] * x1_ref[...], g_ref[...], b_ref[...])
        x1_ref[...] = y
        h2_ref[...] = (y * (1.0 + sc_ref[0]) + sh_ref[0]).astype(BF16)


def outproj_ln(o_dsa, o_sb, g_sb, w_out, x, gate, ln_g, ln_b, scale, shift, seq_len, tm=512, tn=256):
    M, D = x.shape
    wd, ws = o_dsa.shape[1], o_sb.shape[1]
    B = gate.shape[0]
    tm, tn = min(tm, seq_len), min(tn, D)

    def per_batch(i, j):
        return (i * tm // seq_len, 0, 0)

    row = lambda i, j: (i, 0)
    shared = lambda i, j: (0, 0)
    once = pl.Buffered(1)
    return pl.pallas_call(
        functools.partial(_outproj_ln_kernel, tn=tn),
        grid=(M // tm, D // tn),
        in_specs=[
            pl.BlockSpec((tm, wd), row),
            pl.BlockSpec((tm, ws), row, pipeline_mode=once),
            pl.BlockSpec((1, ws), shared),
            pl.BlockSpec((wd + ws, tn), lambda i, j: (0, j)),
            pl.BlockSpec((tm, D), row, pipeline_mode=once),
            pl.BlockSpec((1, 1, D), per_batch),
            pl.BlockSpec((1, D), shared),
            pl.BlockSpec((1, D), shared),
            pl.BlockSpec((1, 1, D), per_batch),
            pl.BlockSpec((1, 1, D), per_batch),
        ],
        out_specs=[pl.BlockSpec((tm, D), row), pl.BlockSpec((tm, D), row)],
        out_shape=[jax.ShapeDtypeStruct((M, D), F32), jax.ShapeDtypeStruct((M, D), BF16)],
        scratch_shapes=[pltpu.VMEM((tm, wd + ws), BF16)],
        compiler_params=pltpu.CompilerParams(
            dimension_semantics=("arbitrary", "arbitrary"), vmem_limit_bytes=VMEM_LIMIT_WIDE_ROWS_BYTES),
        name="outproj_ln",
    )(o_dsa, o_sb, g_sb.reshape(1, ws), w_out, x, gate.reshape(B, 1, D), ln_g.reshape(1, D),
      ln_b.reshape(1, D), scale.reshape(B, 1, D), shift.reshape(B, 1, D))


def _scatter_rows_kernel(pos_ref, pad_end_ref, cnt_ref, h_ref, xg_ref, zero_ref, sem, *, tm, tt, top_k,
                         n_exp):
    i = pl.program_id(0)

    def fill_copy(e):
        start = pl.multiple_of(pad_end_ref[e] - tm, tm)
        return pltpu.make_async_copy(zero_ref, xg_ref.at[pl.ds(start, tm)], sem.at[0])

    @pl.when(i == 0)
    def _():
        zero_ref[...] = jnp.zeros(zero_ref.shape, zero_ref.dtype)

        def start_fill(e, c):
            @pl.when(cnt_ref[e] % tm != 0)
            def _():
                fill_copy(e).start()
            return c

        def wait_fill(e, c):
            @pl.when(cnt_ref[e] % tm != 0)
            def _():
                fill_copy(e).wait()
            return c

        def tail_copy(r):
            return pltpu.make_async_copy(
                zero_ref, xg_ref.at[pl.ds(pl.multiple_of(r * tm, tm), tm)], sem.at[0])

        def start_tail(r, c):
            tail_copy(r).start()
            return c

        def wait_tail(r, c):
            tail_copy(r).wait()
            return c

        n_used = pad_end_ref[n_exp - 1] // tm
        n_blocks = xg_ref.shape[0] // tm
        lax.fori_loop(0, n_exp, start_fill, 0)
        lax.fori_loop(n_used, n_blocks, start_tail, 0)
        lax.fori_loop(0, n_exp, wait_fill, 0)
        lax.fori_loop(n_used, n_blocks, wait_tail, 0)

    base = i * tt

    def row_copy(t, k):
        return pltpu.make_async_copy(h_ref.at[t], xg_ref.at[pos_ref[(base + t) * top_k + k]], sem.at[1])

    def start_rows(t, c):
        for k in range(top_k):
            row_copy(t, k).start(priority=k % 2)
        return c

    def wait_rows(t, c):
        for k in range(top_k):
            row_copy(t, k).wait()
        return c

    lax.fori_loop(0, tt, start_rows, 0)
    lax.fori_loop(0, tt, wait_rows, 0)


def scatter_rows(pos, pad_end, counts, h_slabs, n_rows, tm, tt=256):
    n_tok, P, _ = h_slabs.shape
    dtype = h_slabs.dtype
    tt = min(tt, n_tok)
    kernel = functools.partial(_scatter_rows_kernel, tm=tm, tt=tt, top_k=TOP_K_EXPERTS, n_exp=N_EXPERTS)
    return pl.pallas_call(
        kernel,
        grid_spec=pltpu.PrefetchScalarGridSpec(
            num_scalar_prefetch=3,
            grid=(n_tok // tt,),
            in_specs=[pl.BlockSpec((tt, P, LANES), lambda i, p, e, c: (i, 0, 0))],
            out_specs=pl.BlockSpec(memory_space=pl.ANY),
            scratch_shapes=[pltpu.VMEM((tm, P, LANES), dtype), pltpu.SemaphoreType.DMA((2,))],
        ),
        out_shape=jax.ShapeDtypeStruct((n_rows, P, LANES), dtype),
        compiler_params=_cparams(1),
        name="scatter_rows",
    )(pos, pad_end, counts, h_slabs)


def _expert_changed(blk_e_ref, r):
    return (r == 0) | (blk_e_ref[r] != blk_e_ref[jnp.maximum(r - 1, 0)])


def _moe_up_kernel(blk_e_ref, n_used_ref, x_ref, wg_ref, wl_ref, bg_ref, bl_ref, o_ref, wg_bf, wl_bf):
    r = pl.program_id(1)
    used = r < n_used_ref[0]

    @pl.when(used & _expert_changed(blk_e_ref, r))
    def _():
        wg_bf[...] = wg_ref[0].astype(BF16)
        wl_bf[...] = wl_ref[0].astype(BF16)

    @pl.when(used)
    def _():
        x = x_ref[...]
        glu = jnp.dot(x, wg_bf[...], preferred_element_type=F32) + bg_ref[0]
        lin = jnp.dot(x, wl_bf[...], preferred_element_type=F32) + bl_ref[0]
        glu = jnp.minimum(glu, SWIGLU_LIMIT)
        lin = jnp.clip(lin, -SWIGLU_LIMIT, SWIGLU_LIMIT)
        act = glu * (1.0 / (1.0 + jnp.exp(-SWIGLU_ALPHA * glu))) * (lin + 1.0)
        o_ref[...] = act.astype(o_ref.dtype)

    @pl.when(jnp.logical_not(used))
    def _():
        o_ref[...] = jnp.zeros(o_ref.shape, o_ref.dtype)


def _moe_down_kernel(blk_e_ref, n_used_ref, a_ref, wd_ref, bd_ref, o_ref, wd_bf):
    r = pl.program_id(1)
    used = r < n_used_ref[0]

    @pl.when(used & _expert_changed(blk_e_ref, r))
    def _():
        wd_bf[...] = wd_ref[0].astype(BF16)

    @pl.when(used)
    def _():
        out = jnp.dot(a_ref[...], wd_bf[...], preferred_element_type=F32) + bd_ref[0]
        for s in range(o_ref.shape[1]):
            o_ref[:, s, :] = out[:, s * LANES:(s + 1) * LANES]

    @pl.when(jnp.logical_not(used))
    def _():
        o_ref[...] = jnp.zeros(o_ref.shape, o_ref.dtype)


def moe_experts(xg, blk_e, n_used, w_up, b_up, w_down, b_down, tm, tn_up=512, tn_down=2048):
    R, D = xg.shape
    n_exp, d_exp = w_down.shape[0], w_down.shape[1]
    tn_up, tn_down = min(tn_up, d_exp), min(tn_down, D)
    n_up = d_exp // tn_up

    def row(r, n):
        return jnp.minimum(r, n[0] - 1)

    def expert(r, e, n):
        return e[row(r, n)]

    b_up3 = b_up.reshape(n_exp, 1, 2 * d_exp)
    act = pl.pallas_call(
        _moe_up_kernel,
        grid_spec=pltpu.PrefetchScalarGridSpec(
            num_scalar_prefetch=2,
            grid=(n_up, R // tm),
            in_specs=[
                pl.BlockSpec((tm, D), lambda c, r, e, n: (row(r, n), 0)),
                pl.BlockSpec((1, D, tn_up), lambda c, r, e, n: (expert(r, e, n), 0, c)),
                pl.BlockSpec((1, D, tn_up), lambda c, r, e, n: (expert(r, e, n), 0, n_up + c)),
                pl.BlockSpec((1, 1, tn_up), lambda c, r, e, n: (expert(r, e, n), 0, c)),
                pl.BlockSpec((1, 1, tn_up), lambda c, r, e, n: (expert(r, e, n), 0, n_up + c)),
            ],
            out_specs=pl.BlockSpec((tm, tn_up), lambda c, r, e, n: (r, c)),
            scratch_shapes=[pltpu.VMEM((D, tn_up), BF16), pltpu.VMEM((D, tn_up), BF16)],
        ),
        out_shape=jax.ShapeDtypeStruct((R, d_exp), BF16),
        compiler_params=_cparams(2),
        name="moe_up",
    )(blk_e, n_used, xg, w_up, w_up, b_up3, b_up3)
    return pl.pallas_call(
        _moe_down_kernel,
        grid_spec=pltpu.PrefetchScalarGridSpec(
            num_scalar_prefetch=2,
            grid=(D // tn_down, R // tm),
            in_specs=[
                pl.BlockSpec((tm, d_exp), lambda c, r, e, n: (row(r, n), 0)),
                pl.BlockSpec((1, d_exp, tn_down), lambda c, r, e, n: (expert(r, e, n), 0, c)),
                pl.BlockSpec((1, 1, tn_down), lambda c, r, e, n: (expert(r, e, n), 0, c)),
            ],
            out_specs=pl.BlockSpec((tm, tn_down // LANES, LANES), lambda c, r, e, n: (r, c, 0)),
            scratch_shapes=[pltpu.VMEM((d_exp, tn_down), BF16)],
        ),
        out_shape=jax.ShapeDtypeStruct((R, D // LANES, LANES), F32),
        compiler_params=_cparams(2),
        name="moe_down",
    )(blk_e, n_used, act, w_down, b_down.reshape(n_exp, 1, D))


def _combine_ln_kernel(pos_ref, gates_ref, rows_ref, x_ref, gate_ref, g_ref, b_ref, o_ref,
                       buf0, buf1, ysl_ref, y_ref, sem, *, tt, top_k):
    i = pl.program_id(0)
    n_blocks = pl.num_programs(0)

    def row_copy(blk, buf, slot, t, k):
        src = rows_ref.at[pos_ref[(blk * tt + t) * top_k + k]]
        return pltpu.make_async_copy(src, buf.at[k * tt + t], sem.at[slot])

    def start_block(blk, buf, slot):
        def body(t, c):
            for k in range(top_k):
                row_copy(blk, buf, slot, t, k).start(priority=k % 2)
            return c
        lax.fori_loop(0, tt, body, 0)

    def wait_block(blk, buf, slot):
        def body(t, c):
            for k in range(top_k):
                row_copy(blk, buf, slot, t, k).wait()
            return c
        lax.fori_loop(0, tt, body, 0)

    def combine(buf):
        def per_token(t, c):
            base = (i * tt + t) * top_k
            acc = gates_ref[base] * buf[t]
            for k in range(1, top_k):
                acc = acc + gates_ref[base + k] * buf[k * tt + t]
            ysl_ref[t] = acc
            return c
        lax.fori_loop(0, tt, per_token, 0, unroll=4)
        for s in range(buf.shape[1]):
            y_ref[:, s * LANES:(s + 1) * LANES] = ysl_ref[:, s, :]

    even = i % 2 == 0

    @pl.when(i == 0)
    def _():
        start_block(0, buf0, 0)

    @pl.when((i + 1 < n_blocks) & even)
    def _():
        start_block(i + 1, buf1, 1)

    @pl.when((i + 1 < n_blocks) & jnp.logical_not(even))
    def _():
        start_block(i + 1, buf0, 0)

    @pl.when(even)
    def _():
        wait_block(i, buf0, 0)
        combine(buf0)

    @pl.when(jnp.logical_not(even))
    def _():
        wait_block(i, buf1, 1)
        combine(buf1)

    o_ref[...] = _layer_norm(DEEPNORM_ALPHA * x_ref[...] + gate_ref[0] * y_ref[...], g_ref[...], b_ref[...])


def combine_ln(pos, gates, out_rows, x, gate, ln_g, ln_b, seq_len, tt=128):
    n_tok, D = x.shape
    B = gate.shape[0]
    tt = min(tt, seq_len)
    P = out_rows.shape[1]
    kernel = functools.partial(_combine_ln_kernel, tt=tt, top_k=TOP_K_EXPERTS)
    return pl.pallas_call(
        kernel,
        grid_spec=pltpu.PrefetchScalarGridSpec(
            num_scalar_prefetch=2,
            grid=(n_tok // tt,),
            in_specs=[
                pl.BlockSpec(memory_space=pl.ANY),
                pl.BlockSpec((tt, D), lambda i, p, q: (i, 0)),
                pl.BlockSpec((1, 1, D), lambda i, p, q: (i * tt // seq_len, 0, 0)),
                pl.BlockSpec((1, D), lambda i, p, q: (0, 0)),
                pl.BlockSpec((1, D), lambda i, p, q: (0, 0)),
            ],
            out_specs=pl.BlockSpec((tt, D), lambda i, p, q: (i, 0)),
            scratch_shapes=[pltpu.VMEM((TOP_K_EXPERTS * tt, P, LANES), F32),
                            pltpu.VMEM((TOP_K_EXPERTS * tt, P, LANES), F32),
                            pltpu.VMEM((tt, P, LANES), F32),
                            pltpu.VMEM((tt, D), F32),
                            pltpu.SemaphoreType.DMA((2,))],
        ),
        out_shape=jax.ShapeDtypeStruct((n_tok, D), F32),
        compiler_params=_cparams(1),
        name="combine_ln",
    )(pos, gates, out_rows, x, gate.reshape(B, 1, D), ln_g.reshape(1, D), ln_b.reshape(1, D))


def route(logits, tm):
    n_tok = logits.shape[0]
    top_logit, top_e = lax.top_k(logits, TOP_K_EXPERTS)
    gates = jax.nn.softmax(top_logit, axis=-1)
    n_asg = n_tok * TOP_K_EXPERTS
    e_flat = top_e.reshape(n_asg)
    onehot = (e_flat[:, None] == jnp.arange(N_EXPERTS, dtype=e_flat.dtype)[None, :]).astype(jnp.int32)
    running = jnp.cumsum(onehot, axis=0)
    rank = jnp.take_along_axis(running, e_flat[:, None], axis=1)[:, 0] - 1
    counts = running[-1]
    padded = (counts + tm - 1) // tm * tm
    pad_end = jnp.cumsum(padded)
    pad_start = pad_end - padded
    pos = (pad_start[e_flat] + rank).astype(jnp.int32)
    n_blocks = -(-n_asg // tm) + N_EXPERTS
    blk_e = jnp.minimum(jnp.searchsorted(pad_end, jnp.arange(n_blocks) * tm, side='right'),
                        N_EXPERTS - 1).astype(jnp.int32)
    n_used = (pad_end[-1] // tm).astype(jnp.int32).reshape(1)
    return (gates, pos, blk_e, n_used, pad_end.astype(jnp.int32), counts.astype(jnp.int32),
            n_blocks * tm)


def router_logits(h2, w_router, b_router):
    D = h2.shape[1]
    wr = jnp.zeros((D, LANES), BF16).at[:, :N_EXPERTS].set(w_router.astype(BF16))
    return matmul(h2, wr, F32, tm=512, tn=LANES)[:, :N_EXPERTS] + b_router.astype(F32)


def moe_block(h2, x1, gate_f, ln_g, ln_b, w_router, b_router, w_up, b_up, w_down, b_down, seq_len, tm=512):
    n_tok, D = h2.shape
    logits = router_logits(h2, w_router, b_router)
    gates, pos, blk_e, n_used, pad_end, counts, n_rows = route(logits, tm)
    xg = scatter_rows(pos, pad_end, counts, h2.reshape(n_tok, D // LANES, LANES), n_rows, tm)
    out_rows = moe_experts(xg.reshape(n_rows, D), blk_e, n_used, w_up, b_up, w_down, b_down, tm)
    return combine_ln(pos, gates.reshape(-1), out_rows, x1, gate_f, ln_g, ln_b, seq_len)


def _layer(x, c, w_ada, b_ada, w_in, kv_norm_g, w_uk, w_uv, grp_norm_dsa, grp_norm_sb, w_out,
           ln1_g, ln1_b, w_router, b_router, w_up, b_up, w_down, b_down, ln2_g, ln2_b):
    B, S, D = x.shape
    n_tok = B * S
    n_dsa = w_uk.shape[0]
    w_qa = n_dsa * HEAD_DIM
    w_qidx = IDX_HEADS * IDX_DIM
    w_sb = grp_norm_sb.shape[0]
    n_sb = w_sb // HEAD_DIM

    mod = ada_modulation(c, w_ada, b_ada)
    shift_a, scale_a, gate_a, shift_f, scale_f, gate_f = jnp.split(mod, 6, axis=-1)

    o1 = w_qa
    o2 = o1 + KV_LATENT
    o3 = o2 + w_qidx
    o4 = o3 + IDX_DIM
    o5 = o4 + IDX_HEADS
    w_main = jnp.concatenate([w_in[:, o2:o3], w_in[:, :o1], w_in[:, o5:]], axis=1).astype(BF16)
    n_small = KV_LATENT + IDX_DIM + IDX_HEADS
    n_small_pad = KV_LATENT + IDX_DIM + LANES
    w_small = jnp.concatenate(
        [w_in[:, o1:o2], w_in[:, o3:o5], jnp.zeros((D, n_small_pad - n_small), F32)], axis=1).astype(BF16)

    h = modulate(x, scale_a, shift_a).reshape(n_tok, D)
    proj = matmul(h, w_main, BF16, tm=512, tn=512).reshape(B, S, w_main.shape[1])
    small = matmul(h, w_small, F32, tm=512, tn=n_small_pad)
    ckv, kidx = kv_prep(small, kv_norm_g)

    sb_w = SB_HEADS_PER_STEP * HEAD_DIM
    o_dsa = dsa_attention(
        qidx=(proj, 0), kidx=kidx.reshape(B, S, IDX_DIM),
        widx=(small.reshape(B, S, n_small_pad), (KV_LATENT + IDX_DIM) // LANES),
        qa=(proj, w_qidx // w_qa), ckv=ckv.reshape(B, S, KV_LATENT),
        w_uk=w_uk.astype(BF16), w_uv=w_uv.astype(BF16), g=grp_norm_dsa)
    sb0 = w_qidx + w_qa
    o_sb = stick_breaking_attention((proj, sb0 // sb_w), (proj, (sb0 + w_sb) // sb_w),
                                    (proj, (sb0 + 2 * w_sb) // sb_w), n_sb)

    x1, h2 = outproj_ln(o_dsa.reshape(n_tok, w_qa), o_sb.reshape(n_tok, w_sb), grp_norm_sb,
                        w_out.astype(BF16), x.reshape(n_tok, D), gate_a, ln1_g, ln1_b, scale_f, shift_f, S)
    out = moe_block(h2, x1, gate_f, ln2_g, ln2_b, w_router, b_router, w_up, b_up, w_down, b_down, S)
    return out.reshape(B, S, D)


def kernel(x, c, w_ada, b_ada, w_in, kv_norm_g, w_uk, w_uv, grp_norm_dsa, grp_norm_sb, w_out, ln1_g, ln1_b, w_router, b_router, w_up, b_up, w_down, b_down, ln2_g, ln2_b):
    return _layer(x, c, w_ada[0], b_ada[0], w_in[0], kv_norm_g[0], w_uk[0], w_uv[0],
                  grp_norm_dsa[0], grp_norm_sb[0], w_out[0], ln1_g[0], ln1_b[0], w_router[0],
                  b_router[0], w_up[0], b_up[0], w_down[0], b_down[0], ln2_g[0], ln2_b[0])
```

```python
import functools
import math

import jax
import jax.numpy as jnp
from jax import lax
from jax.experimental import pallas as pl
from jax.experimental.pallas import tpu as pltpu

F32 = jnp.float32
BF16 = jnp.bfloat16

HEAD_DIM = 128
KV_LATENT = 512
IDX_HEADS = 32
IDX_DIM = 128
TOPK_MAX = 256
N_EXPERTS = 32
TOP_K_EXPERTS = 4
SWIGLU_ALPHA = 1.702
SWIGLU_LIMIT = 7.0
DEPTH = 1
DEEPNORM_ALPHA = (2 * DEPTH) ** 0.25
EPS = 1e-5
INDEX_SCALE = (IDX_HEADS * IDX_DIM) ** -0.5
ATTN_SCALE = HEAD_DIM ** -0.5
LOG2_E = math.log2(math.e)

LANES = 128
VMEM_LIMIT_BYTES = 56 * 1024 * 1024
VMEM_LIMIT_WIDE_ROWS_BYTES = 60 * 1024 * 1024

MASKED_LOGIT = -1e30
F32_EXP_UNDERFLOW = 104.0

NT_DIMS = (((1,), (1,)), ((), ()))


def _cparams(n_axes):
    return pltpu.CompilerParams(
        dimension_semantics=("arbitrary",) * n_axes, vmem_limit_bytes=VMEM_LIMIT_BYTES)


def _split_bf16(v):
    hi = v.astype(BF16)
    lo = (v - hi.astype(F32)).astype(BF16)
    return hi, lo


def _lane_tile(v, width):
    return jnp.concatenate([v] * (width // LANES), axis=1)


def _ada_kernel(c_ref, w_ref, b_ref, o_ref):
    c = c_ref[...]
    s = c * (1.0 / (1.0 + jnp.exp(-c)))
    s_hi, s_lo = _split_bf16(s)
    w_hi, w_lo = _split_bf16(w_ref[...])
    acc = jnp.dot(s_hi, w_hi, preferred_element_type=F32)
    acc += jnp.dot(s_hi, w_lo, preferred_element_type=F32)
    acc += jnp.dot(s_lo, w_hi, preferred_element_type=F32)
    o_ref[...] = acc + b_ref[...]


def ada_modulation(c, w_ada, b_ada, tn=512):
    B, D = c.shape
    N = w_ada.shape[1]
    rows = 8
    c_pad = jnp.zeros((rows, D), F32).at[:B].set(c)
    out = pl.pallas_call(
        _ada_kernel,
        grid=(N // tn,),
        in_specs=[pl.BlockSpec((rows, D), lambda j: (0, 0)),
                  pl.BlockSpec((D, tn), lambda j: (0, j)),
                  pl.BlockSpec((1, tn), lambda j: (0, j))],
        out_specs=pl.BlockSpec((rows, tn), lambda j: (0, j)),
        out_shape=jax.ShapeDtypeStruct((rows, N), F32),
        compiler_params=_cparams(1),
        name="ada_modulation",
    )(c_pad, w_ada, b_ada.reshape(1, N))
    return out[:B]


def _modulate_kernel(x_ref, sc_ref, sh_ref, o_ref):
    o_ref[0] = (x_ref[0] * (1.0 + sc_ref[0]) + sh_ref[0]).astype(o_ref.dtype)


def modulate(x, scale, shift, tm=512):
    B, S, D = x.shape
    tm = min(tm, S)
    vec = pl.BlockSpec((1, 1, D), lambda b, i: (b, 0, 0))
    return pl.pallas_call(
        _modulate_kernel,
        grid=(B, S // tm),
        in_specs=[pl.BlockSpec((1, tm, D), lambda b, i: (b, i, 0)), vec, vec],
        out_specs=pl.BlockSpec((1, tm, D), lambda b, i: (b, i, 0)),
        out_shape=jax.ShapeDtypeStruct((B, S, D), BF16),
        compiler_params=_cparams(2),
        name="modulate",
    )(x, scale.reshape(B, 1, D), shift.reshape(B, 1, D))


def _matmul_kernel(a_ref, b_ref, o_ref):
    o_ref[...] = jnp.dot(a_ref[...], b_ref[...], preferred_element_type=F32).astype(o_ref.dtype)


def matmul(a, b, out_dtype, tm=512, tn=512):
    M, K = a.shape
    N = b.shape[1]
    tm, tn = min(tm, M), min(tn, N)
    return pl.pallas_call(
        _matmul_kernel,
        grid=(M // tm, N // tn),
        in_specs=[pl.BlockSpec((tm, K), lambda i, j: (i, 0)),
                  pl.BlockSpec((K, tn), lambda i, j: (0, j))],
        out_specs=pl.BlockSpec((tm, tn), lambda i, j: (i, j)),
        out_shape=jax.ShapeDtypeStruct((M, N), out_dtype),
        compiler_params=_cparams(2),
        name="matmul",
    )(a, b)


def _kv_prep_kernel(s_ref, g_ref, ckv_ref, kidx_ref):
    x = s_ref[:, :KV_LATENT]
    ms = jnp.mean(x * x, axis=-1, keepdims=True)
    ckv_ref[...] = (x * lax.rsqrt(ms + EPS) * g_ref[...]).astype(BF16)
    kidx_ref[...] = s_ref[:, KV_LATENT:KV_LATENT + IDX_DIM].astype(BF16)


def kv_prep(small, g, tm=512):
    M, W = small.shape
    tm = min(tm, M)
    return pl.pallas_call(
        _kv_prep_kernel,
        grid=(M // tm,),
        in_specs=[pl.BlockSpec((tm, W), lambda i: (i, 0)), pl.BlockSpec((1, KV_LATENT), lambda i: (0, 0))],
        out_specs=[pl.BlockSpec((tm, KV_LATENT), lambda i: (i, 0)), pl.BlockSpec((tm, IDX_DIM), lambda i: (i, 0))],
        out_shape=[jax.ShapeDtypeStruct((M, KV_LATENT), BF16), jax.ShapeDtypeStruct((M, IDX_DIM), BF16)],
        compiler_params=_cparams(1),
        name="kv_prep",
    )(small, g.reshape(1, KV_LATENT))


def _sortable_key(v):
    bits = pltpu.bitcast(v + 0.0, jnp.int32)
    return bits ^ (lax.shift_right_arithmetic(bits, 31) & jnp.int32(0x7FFFFFFF))


def _dsa_kernel(qidx_ref, kidx_ref, widx_ref, qa_ref, ckv_ref, wuk_ref, wuv_ref, g_ref, o_ref,
                key_ref, wb_ref, q2_ref, qlat_ref, lg_ref, bias_ref, p_ref, acc_ref, m_ref, l_ref,
                alpha_ref, slope_ref, tie_end_ref, obuf_ref, *, tq, tk_score, tk, k_sel, n_heads, n_idx):
    s_len = key_ref.shape[1]
    pos_bits = max(1, (s_len - 1).bit_length())
    i = pl.program_id(1)
    t0 = i * tq
    n_att = (t0 + tq - 1) // tk + 1
    n_score = n_att * (tk // tk_score)

    w = widx_ref[0]
    for h in range(n_idx):
        wb_ref[h] = jnp.broadcast_to(w[:, h:h + 1], (tq, LANES))
        q2_ref[h * tq:(h + 1) * tq, :] = qidx_ref[0, :, h * IDX_DIM:(h + 1) * IDX_DIM]
    t_ids_s = t0 + lax.broadcasted_iota(jnp.int32, (tq, tk_score), 0)
    lane_ids_s = lax.broadcasted_iota(jnp.int32, (tq, tk_score), 1)

    def score_chunk(j, carry):
        s0 = pl.multiple_of(j * tk_score, tk_score)
        kb = kidx_ref[0, pl.ds(s0, tk_score), :]
        lg = lax.dot_general(q2_ref[...], kb, NT_DIMS, preferred_element_type=F32)
        r = jnp.maximum(lg, 0.0).reshape(n_idx, tq, tk_score)
        parts = [jnp.sum(r[:, :, c * LANES:(c + 1) * LANES] * wb_ref[...], axis=0)
                 for c in range(tk_score // LANES)]
        sc = jnp.concatenate(parts, axis=1) * INDEX_SCALE
        sc = jnp.where(s0 + lane_ids_s <= t_ids_s, sc, -jnp.inf)
        key_ref[:, pl.ds(s0, tk_score)] = _sortable_key(sc)
        return carry

    lax.fori_loop(0, n_score, score_chunk, 0)

    lane_ids = lax.broadcasted_iota(jnp.int32, (tq, tk), 1)

    def count_where(preds):
        def chunk(j, cnts):
            s0 = pl.multiple_of(j * tk, tk)
            keys = key_ref[:, pl.ds(s0, tk)]
            s_ids = s0 + lane_ids
            hits = [jnp.where(pred(keys, s_ids), 1.0, 0.0) for pred in preds]
            return tuple(cnt + sum(hit[:, c * LANES:(c + 1) * LANES] for c in range(tk // LANES))
                         for cnt, hit in zip(cnts, hits))
        zero = jnp.zeros((tq, LANES), F32)
        cnts = lax.fori_loop(0, n_att, chunk, (zero,) * len(preds))
        return [jnp.sum(cnt, axis=1, keepdims=True) for cnt in cnts]

    def bisect(it, prefix):
        cand = prefix + lax.shift_left(jnp.int32(1), 31 - it)
        cand_t = _lane_tile(cand, tk)
        (total,) = count_where([lambda keys, s_ids: keys >= cand_t])
        return jnp.where(total >= float(k_sel), cand, prefix)

    thr = lax.fori_loop(0, 32, bisect, jnp.full((tq, LANES), -2 ** 31, jnp.int32))
    thr_t = _lane_tile(thr, tk)

    n_above, n_at_least = count_where([lambda keys, s_ids: keys > thr_t,
                                       lambda keys, s_ids: keys >= thr_t])
    need = float(k_sel) - n_above
    tie_end_ref[...] = jnp.full((tq, LANES), s_len, jnp.int32)

    @pl.when(jnp.max(n_at_least) > float(k_sel))
    def _():
        def grow(it, p):
            cand = p | lax.shift_left(jnp.int32(1), pos_bits - 1 - it)
            cand_t = _lane_tile(cand, tk)
            (below,) = count_where([lambda keys, s_ids: (keys == thr_t) & (s_ids < cand_t)])
            return jnp.where(below < need, cand, p)
        tie_end_ref[...] = lax.fori_loop(0, pos_bits, grow, jnp.zeros((tq, LANES), jnp.int32))

    for h in range(n_heads):
        ql = jnp.dot(qa_ref[0, :, h * HEAD_DIM:(h + 1) * HEAD_DIM], wuk_ref[h], preferred_element_type=F32)
        qlat_ref[h * tq:(h + 1) * tq, :] = (ql * (ATTN_SCALE * LOG2_E)).astype(BF16)
    head_no = lax.broadcasted_iota(jnp.int32, slope_ref.shape, 0).astype(F32)
    slope_ref[...] = jnp.exp2(-8.0 * (head_no + 1.0) / n_heads) * LOG2_E
    m_ref[...] = jnp.full(m_ref.shape, MASKED_LOGIT, F32)
    l_ref[...] = jnp.zeros(l_ref.shape, F32)
    acc_ref[...] = jnp.zeros(acc_ref.shape, F32)
    t_ids = t0 + lax.broadcasted_iota(jnp.int32, (tq, tk), 0)
    col_ids = lax.broadcasted_iota(jnp.int32, (1, tk), 1)

    def attend_chunk(j, carry):
        s0 = pl.multiple_of(j * tk, tk)
        cb = ckv_ref[0, pl.ds(s0, tk), :]
        lg_ref[...] = lax.dot_general(qlat_ref[...], cb, NT_DIMS, preferred_element_type=F32)
        keys = key_ref[:, pl.ds(s0, tk)]
        s_ids = s0 + lane_ids
        sel = (keys > thr_t) | ((keys == thr_t) & (s_ids <= _lane_tile(tie_end_ref[...], tk)))
        sel = sel & (s_ids <= t_ids)
        bias_ref[...] = jnp.where(sel, 0.0, MASKED_LOGIT)
        rel = (s0 - t0 + col_ids).astype(F32)

        def head(h, c2):
            rows = pl.ds(pl.multiple_of(h * tq, tq), tq)
            x = lg_ref[rows, :] + _lane_tile(slope_ref[h], tk) * rel + bias_ref[...]
            m_old = m_ref[rows, :]
            m_new = jnp.maximum(m_old, jnp.max(x, axis=1, keepdims=True))
            alpha = jnp.exp2(m_old - m_new)
            p = jnp.exp2(x - _lane_tile(m_new, tk))
            l_ref[rows, :] = alpha * l_ref[rows, :] + jnp.sum(p, axis=1, keepdims=True)
            m_ref[rows, :] = m_new
            alpha_ref[rows, :] = alpha
            p_ref[rows, :] = p.astype(BF16)
            return c2

        lax.fori_loop(0, n_heads, head, 0, unroll=True)
        lat = acc_ref.shape[1]
        acc_ref[...] = (acc_ref[...] * _lane_tile(alpha_ref[...], lat)
                        + jnp.dot(p_ref[...], cb, preferred_element_type=F32))
        return carry

    lax.fori_loop(0, n_att, attend_chunk, 0)

    ss = jnp.zeros((tq, 1), F32)
    for h in range(n_heads):
        rows = slice(h * tq, (h + 1) * tq)
        inv_l = _lane_tile(1.0 / l_ref[rows, :], acc_ref.shape[1])
        oh = jnp.dot((acc_ref[rows, :] * inv_l).astype(BF16), wuv_ref[h],
                     preferred_element_type=F32)
        obuf_ref[:, h * HEAD_DIM:(h + 1) * HEAD_DIM] = oh
        ss = ss + jnp.sum(oh * oh, axis=1, keepdims=True)
    inv = lax.rsqrt(ss * (1.0 / (n_heads * HEAD_DIM)) + EPS)
    o_ref[0] = (obuf_ref[...] * inv * g_ref[...]).astype(o_ref.dtype)


def dsa_attention(qidx, kidx, widx, qa, ckv, w_uk, w_uv, g, tq=128, tk_score=512, tk=512):
    qidx_arr, qidx_blk = qidx
    qa_arr, qa_blk = qa
    widx_arr, widx_blk = widx
    B, S, lat = ckv.shape
    n_heads = w_uk.shape[0]
    n_idx = IDX_HEADS
    tq, tk = min(tq, S), min(tk, S)
    tk_score = min(tk_score, tk)
    k_sel = min(TOPK_MAX, S // 4)
    width = n_heads * HEAD_DIM
    kernel = functools.partial(_dsa_kernel, tq=tq, tk_score=tk_score, tk=tk, k_sel=k_sel,
                               n_heads=n_heads, n_idx=n_idx)
    once = pl.Buffered(1)
    return pl.pallas_call(
        kernel,
        grid=(B, S // tq),
        in_specs=[
            pl.BlockSpec((1, tq, n_idx * IDX_DIM), lambda b, i: (b, i, qidx_blk)),
            pl.BlockSpec((1, S, IDX_DIM), lambda b, i: (b, 0, 0), pipeline_mode=once),
            pl.BlockSpec((1, tq, LANES), lambda b, i: (b, i, widx_blk)),
            pl.BlockSpec((1, tq, width), lambda b, i: (b, i, qa_blk)),
            pl.BlockSpec((1, S, lat), lambda b, i: (b, 0, 0), pipeline_mode=once),
            pl.BlockSpec((n_heads, HEAD_DIM, lat), lambda b, i: (0, 0, 0), pipeline_mode=once),
            pl.BlockSpec((n_heads, lat, HEAD_DIM), lambda b, i: (0, 0, 0), pipeline_mode=once),
            pl.BlockSpec((1, width), lambda b, i: (0, 0)),
        ],
        out_specs=pl.BlockSpec((1, tq, width), lambda b, i: (b, i, 0)),
        out_shape=jax.ShapeDtypeStruct((B, S, width), BF16),
        scratch_shapes=[
            pltpu.VMEM((tq, S), jnp.int32),
            pltpu.VMEM((n_idx, tq, LANES), F32),
            pltpu.VMEM((n_idx * tq, IDX_DIM), BF16),
            pltpu.VMEM((n_heads * tq, lat), BF16),
            pltpu.VMEM((n_heads * tq, tk), F32),
            pltpu.VMEM((tq, tk), F32),
            pltpu.VMEM((n_heads * tq, tk), BF16),
            pltpu.VMEM((n_heads * tq, lat), F32),
            pltpu.VMEM((n_heads * tq, LANES), F32),
            pltpu.VMEM((n_heads * tq, LANES), F32),
            pltpu.VMEM((n_heads * tq, LANES), F32),
            pltpu.VMEM((n_heads, 1, LANES), F32),
            pltpu.VMEM((tq, LANES), jnp.int32),
            pltpu.VMEM((tq, width), F32),
        ],
        compiler_params=_cparams(2),
        name="dsa_attention",
    )(qidx_arr, kidx, widx_arr, qa_arr, ckv, w_uk, w_uv, g.reshape(1, width))


def _sb_kernel(q_ref, k_ref, v_ref, o_ref, acc_ref, carry_ref, *, tq, hb):
    i = pl.program_id(2)
    t0 = i * tq
    row = lax.broadcasted_iota(jnp.int32, (tq, tq), 0)
    col = lax.broadcasted_iota(jnp.int32, (tq, tq), 1)
    suffix_ones = jnp.where(row > col, 1.0, 0.0).astype(BF16)
    acc_ref[...] = jnp.zeros(acc_ref.shape, F32)
    carry_ref[...] = jnp.zeros(carry_ref.shape, F32)

    def keep_going(state):
        j, carry_min = state
        return (j >= 0) & (carry_min <= F32_EXP_UNDERFLOW)

    def block(state):
        j, _ = state
        s0 = pl.multiple_of(j * tq, tq)
        strict = (s0 + col) < (t0 + row)
        carry_min = None
        for h in range(hb):
            cols = slice(h * HEAD_DIM, (h + 1) * HEAD_DIM)
            kb = k_ref[0, pl.ds(s0, tq), cols]
            vb = v_ref[0, pl.ds(s0, tq), cols]
            z = lax.dot_general(q_ref[0, :, cols], kb, NT_DIMS, preferred_element_type=F32) * ATTN_SCALE
            sp = jnp.maximum(z, 0.0) + jnp.log(1.0 + jnp.exp(-jnp.abs(z)))
            u = jnp.where(strict, sp, 0.0)
            u_hi, u_lo = _split_bf16(u)
            within = (jnp.dot(u_hi, suffix_ones, preferred_element_type=F32)
                      + jnp.dot(u_lo, suffix_ones, preferred_element_type=F32))
            carry = carry_ref[:, cols]
            a = jnp.where(strict, jnp.exp(z - sp - (_lane_tile(carry, tq) + within)), 0.0)
            acc_ref[:, cols] += jnp.dot(a.astype(BF16), vb, preferred_element_type=F32)
            carry = carry + jnp.sum(u, axis=1, keepdims=True)
            carry_ref[:, cols] = carry
            head_min = jnp.min(carry)
            carry_min = head_min if carry_min is None else jnp.minimum(carry_min, head_min)
        return j - 1, carry_min

    lax.while_loop(keep_going, block, (i, jnp.float32(0.0)))
    o_ref[0] = acc_ref[...]


SB_HEADS_PER_STEP = 8


def stick_breaking_attention(q, k, v, n_heads, tq=256, hb=SB_HEADS_PER_STEP):
    (q_arr, q_blk), (k_arr, k_blk), (v_arr, v_blk) = q, k, v
    B, S, _ = q_arr.shape
    tq = min(tq, S)
    wb = hb * HEAD_DIM
    once = pl.Buffered(1)
    return pl.pallas_call(
        functools.partial(_sb_kernel, tq=tq, hb=hb),
        grid=(B, n_heads // hb, S // tq),
        in_specs=[pl.BlockSpec((1, tq, wb), lambda b, h, i: (b, i, q_blk + h)),
                  pl.BlockSpec((1, S, wb), lambda b, h, i: (b, 0, k_blk + h), pipeline_mode=once),
                  pl.BlockSpec((1, S, wb), lambda b, h, i: (b, 0, v_blk + h), pipeline_mode=once)],
        out_specs=pl.BlockSpec((1, tq, wb), lambda b, h, i: (b, i, h)),
        out_shape=jax.ShapeDtypeStruct((B, S, n_heads * HEAD_DIM), F32),
        scratch_shapes=[pltpu.VMEM((tq, wb), F32), pltpu.VMEM((tq, wb), F32)],
        compiler_params=_cparams(3),
        name="stick_breaking_attention",
    )(q_arr, k_arr, v_arr)


def _layer_norm(v, g, b):
    mu = jnp.mean(v, axis=-1, keepdims=True)
    d = v - mu
    var = jnp.mean(d * d, axis=-1, keepdims=True)
    return d * lax.rsqrt(var + EPS) * g + b


def _outproj_ln_kernel(od_ref, os_ref, gsb_ref, w_ref, x_ref, gate_ref, g_ref, b_ref, sc_ref, sh_ref,
                       wr_ref, x1_ref, h2_ref, lg_ref, a_ref, *, tn):
    j = pl.program_id(1)
    wd = od_ref.shape[1]

    @pl.when(j == 0)
    def _():
        a_ref[:, :wd] = od_ref[...]
        o = os_ref[...]
        ms = jnp.mean(o * o, axis=-1, keepdims=True)
        a_ref[:, wd:] = (o * lax.rsqrt(ms + EPS) * gsb_ref[...]).astype(BF16)

    x1_ref[:, pl.ds(pl.multiple_of(j * tn, tn), tn)] = jnp.dot(
        a_ref[...], w_ref[...], preferred_element_type=F32)

    @pl.when(j == pl.num_programs(1) - 1)
    def _():
        y = _layer_norm(DEEPNORM_ALPHA * x_ref[...] + gate_ref[0] * x1_ref[...], g_ref[...], b_ref[...])
        x1_ref[...] = y
        h2 = (y * (1.0 + sc_ref[0]) + sh_ref[0]).astype(BF16)
        lg_ref[...] = jnp.dot(h2, wr_ref[...], preferred_element_type=F32)
        h2_ref[...] = pltpu.einshape("t(sl)->tsl", h2, s=h2_ref.shape[1])


def outproj_ln(o_dsa, o_sb, g_sb, w_out, x, gate, ln_g, ln_b, scale, shift, w_router, seq_len,
               tm=512, tn=256):
    M, D = x.shape
    wd, ws = o_dsa.shape[1], o_sb.shape[1]
    B = gate.shape[0]
    tm, tn = min(tm, seq_len), min(tn, D)
    n_exp = w_router.shape[1]
    wr = jnp.zeros((D, LANES), BF16).at[:, :n_exp].set(w_router.astype(BF16))

    def per_batch(i, j):
        return (i * tm // seq_len, 0, 0)

    row = lambda i, j: (i, 0)
    shared = lambda i, j: (0, 0)
    once = pl.Buffered(1)
    x1, h2_slabs, logits = pl.pallas_call(
        functools.partial(_outproj_ln_kernel, tn=tn),
        grid=(M // tm, D // tn),
        in_specs=[
            pl.BlockSpec((tm, wd), row),
            pl.BlockSpec((tm, ws), row, pipeline_mode=once),
            pl.BlockSpec((1, ws), shared),
            pl.BlockSpec((wd + ws, tn), lambda i, j: (0, j)),
            pl.BlockSpec((tm, D), row, pipeline_mode=once),
            pl.BlockSpec((1, 1, D), per_batch),
            pl.BlockSpec((1, D), shared),
            pl.BlockSpec((1, D), shared),
            pl.BlockSpec((1, 1, D), per_batch),
            pl.BlockSpec((1, 1, D), per_batch),
            pl.BlockSpec((D, LANES), shared),
        ],
        out_specs=[pl.BlockSpec((tm, D), row),
                   pl.BlockSpec((tm, D // LANES, LANES), lambda i, j: (i, 0, 0)),
                   pl.BlockSpec((tm, LANES), row)],
        out_shape=[jax.ShapeDtypeStruct((M, D), F32),
                   jax.ShapeDtypeStruct((M, D // LANES, LANES), BF16),
                   jax.ShapeDtypeStruct((M, LANES), F32)],
        scratch_shapes=[pltpu.VMEM((tm, wd + ws), BF16)],
        compiler_params=pltpu.CompilerParams(
            dimension_semantics=("arbitrary", "arbitrary"), vmem_limit_bytes=VMEM_LIMIT_WIDE_ROWS_BYTES),
        name="outproj_ln",
    )(o_dsa, o_sb, g_sb.reshape(1, ws), w_out, x, gate.reshape(B, 1, D), ln_g.reshape(1, D),
      ln_b.reshape(1, D), scale.reshape(B, 1, D), shift.reshape(B, 1, D), wr)
    return x1, h2_slabs, logits[:, :n_exp]


def _scatter_rows_kernel(pos_ref, pad_end_ref, cnt_ref, h_ref, xg_ref, zero_ref, sem, *, tm, tt, top_k,
                         n_exp):
    i = pl.program_id(0)

    def fill_copy(e):
        start = pl.multiple_of(pad_end_ref[e] - tm, tm)
        return pltpu.make_async_copy(zero_ref, xg_ref.at[pl.ds(start, tm)], sem.at[0])

    @pl.when(i == 0)
    def _():
        zero_ref[...] = jnp.zeros(zero_ref.shape, zero_ref.dtype)

        def start_fill(e, c):
            @pl.when(cnt_ref[e] % tm != 0)
            def _():
                fill_copy(e).start()
            return c

        def wait_fill(e, c):
            @pl.when(cnt_ref[e] % tm != 0)
            def _():
                fill_copy(e).wait()
            return c

        def tail_copy(r):
            return pltpu.make_async_copy(
                zero_ref, xg_ref.at[pl.ds(pl.multiple_of(r * tm, tm), tm)], sem.at[0])

        def start_tail(r, c):
            tail_copy(r).start()
            return c

        def wait_tail(r, c):
            tail_copy(r).wait()
            return c

        n_used = pad_end_ref[n_exp - 1] // tm
        n_blocks = xg_ref.shape[0] // tm
        lax.fori_loop(0, n_exp, start_fill, 0)
        lax.fori_loop(n_used, n_blocks, start_tail, 0)
        lax.fori_loop(0, n_exp, wait_fill, 0)
        lax.fori_loop(n_used, n_blocks, wait_tail, 0)

    base = i * tt

    def row_copy(t, k):
        return pltpu.make_async_copy(h_ref.at[t], xg_ref.at[pos_ref[(base + t) * top_k + k]], sem.at[1])

    def start_rows(t, c):
        for k in range(top_k):
            row_copy(t, k).start(priority=k % 2)
        return c

    def wait_rows(t, c):
        for k in range(top_k):
            row_copy(t, k).wait()
        return c

    lax.fori_loop(0, tt, start_rows, 0)
    lax.fori_loop(0, tt, wait_rows, 0)


def scatter_rows(pos, pad_end, counts, h_slabs, n_rows, tm, tt=256):
    n_tok, P, _ = h_slabs.shape
    dtype = h_slabs.dtype
    tt = min(tt, n_tok)
    kernel = functools.partial(_scatter_rows_kernel, tm=tm, tt=tt, top_k=TOP_K_EXPERTS, n_exp=N_EXPERTS)
    return pl.pallas_call(
        kernel,
        grid_spec=pltpu.PrefetchScalarGridSpec(
            num_scalar_prefetch=3,
            grid=(n_tok // tt,),
            in_specs=[pl.BlockSpec((tt, P, LANES), lambda i, p, e, c: (i, 0, 0))],
            out_specs=pl.BlockSpec(memory_space=pl.ANY),
            scratch_shapes=[pltpu.VMEM((tm, P, LANES), dtype), pltpu.SemaphoreType.DMA((2,))],
        ),
        out_shape=jax.ShapeDtypeStruct((n_rows, P, LANES), dtype),
        compiler_params=_cparams(1),
        name="scatter_rows",
    )(pos, pad_end, counts, h_slabs)


def _expert_changed(blk_e_ref, r):
    return (r == 0) | (blk_e_ref[r] != blk_e_ref[jnp.maximum(r - 1, 0)])


def _moe_up_kernel(blk_e_ref, n_used_ref, x_ref, wg_ref, wl_ref, bg_ref, bl_ref, o_ref, wg_bf, wl_bf):
    r = pl.program_id(1)
    used = r < n_used_ref[0]

    @pl.when(used & _expert_changed(blk_e_ref, r))
    def _():
        wg_bf[...] = wg_ref[0].astype(BF16)
        wl_bf[...] = wl_ref[0].astype(BF16)

    @pl.when(used)
    def _():
        x = pltpu.einshape("tsl->t(sl)", x_ref[...])
        glu = jnp.dot(x, wg_bf[...], preferred_element_type=F32) + bg_ref[0]
        lin = jnp.dot(x, wl_bf[...], preferred_element_type=F32) + bl_ref[0]
        glu = jnp.minimum(glu, SWIGLU_LIMIT)
        lin = jnp.clip(lin, -SWIGLU_LIMIT, SWIGLU_LIMIT)
        act = glu * (1.0 / (1.0 + jnp.exp(-SWIGLU_ALPHA * glu))) * (lin + 1.0)
        o_ref[...] = act.astype(o_ref.dtype)

    @pl.when(jnp.logical_not(used))
    def _():
        o_ref[...] = jnp.zeros(o_ref.shape, o_ref.dtype)


def _moe_down_kernel(blk_e_ref, n_used_ref, a_ref, wd_ref, bd_ref, o_ref, wd_bf):
    r = pl.program_id(1)
    used = r < n_used_ref[0]

    @pl.when(used & _expert_changed(blk_e_ref, r))
    def _():
        wd_bf[...] = wd_ref[0].astype(BF16)

    @pl.when(used)
    def _():
        out = jnp.dot(a_ref[...], wd_bf[...], preferred_element_type=F32) + bd_ref[0]
        o_ref[...] = pltpu.einshape("t(sl)->tsl", out, s=o_ref.shape[1])

    @pl.when(jnp.logical_not(used))
    def _():
        o_ref[...] = jnp.zeros(o_ref.shape, o_ref.dtype)


def moe_experts(xg, blk_e, n_used, w_up, b_up, w_down, b_down, tm, tn_up=512, tn_down=2048):
    R, P, _ = xg.shape
    D = P * LANES
    n_exp, d_exp = w_down.shape[0], w_down.shape[1]
    tn_up, tn_down = min(tn_up, d_exp), min(tn_down, D)
    n_up = d_exp // tn_up

    def row(r, n):
        return jnp.minimum(r, n[0] - 1)

    def expert(r, e, n):
        return e[row(r, n)]

    b_up3 = b_up.reshape(n_exp, 1, 2 * d_exp)
    act = pl.pallas_call(
        _moe_up_kernel,
        grid_spec=pltpu.PrefetchScalarGridSpec(
            num_scalar_prefetch=2,
            grid=(n_up, R // tm),
            in_specs=[
                pl.BlockSpec((tm, P, LANES), lambda c, r, e, n: (row(r, n), 0, 0)),
                pl.BlockSpec((1, D, tn_up), lambda c, r, e, n: (expert(r, e, n), 0, c)),
                pl.BlockSpec((1, D, tn_up), lambda c, r, e, n: (expert(r, e, n), 0, n_up + c)),
                pl.BlockSpec((1, 1, tn_up), lambda c, r, e, n: (expert(r, e, n), 0, c)),
                pl.BlockSpec((1, 1, tn_up), lambda c, r, e, n: (expert(r, e, n), 0, n_up + c)),
            ],
            out_specs=pl.BlockSpec((tm, tn_up), lambda c, r, e, n: (r, c)),
            scratch_shapes=[pltpu.VMEM((D, tn_up), BF16), pltpu.VMEM((D, tn_up), BF16)],
        ),
        out_shape=jax.ShapeDtypeStruct((R, d_exp), BF16),
        compiler_params=_cparams(2),
        name="moe_up",
    )(blk_e, n_used, xg, w_up, w_up, b_up3, b_up3)
    return pl.pallas_call(
        _moe_down_kernel,
        grid_spec=pltpu.PrefetchScalarGridSpec(
            num_scalar_prefetch=2,
            grid=(D // tn_down, R // tm),
            in_specs=[
                pl.BlockSpec((tm, d_exp), lambda c, r, e, n: (row(r, n), 0)),
                pl.BlockSpec((1, d_exp, tn_down), lambda c, r, e, n: (expert(r, e, n), 0, c)),
                pl.BlockSpec((1, 1, tn_down), lambda c, r, e, n: (expert(r, e, n), 0, c)),
            ],
            out_specs=pl.BlockSpec((tm, tn_down // LANES, LANES), lambda c, r, e, n: (r, c, 0)),
            scratch_shapes=[pltpu.VMEM((d_exp, tn_down), BF16)],
        ),
        out_shape=jax.ShapeDtypeStruct((R, D // LANES, LANES), F32),
        compiler_params=_cparams(2),
        name="moe_down",
    )(blk_e, n_used, act, w_down, b_down.reshape(n_exp, 1, D))


def _combine_ln_kernel(pos_ref, gates_ref, rows_ref, x_ref, gate_ref, g_ref, b_ref, o_ref,
                       buf0, buf1, ysl_ref, y_ref, sem, *, tt, top_k):
    i = pl.program_id(0)
    n_blocks = pl.num_programs(0)

    def row_copy(blk, buf, slot, t, k):
        src = rows_ref.at[pos_ref[(blk * tt + t) * top_k + k]]
        return pltpu.make_async_copy(src, buf.at[k * tt + t], sem.at[slot])

    def start_block(blk, buf, slot):
        def body(t, c):
            for k in range(top_k):
                row_copy(blk, buf, slot, t, k).start(priority=k % 2)
            return c
        lax.fori_loop(0, tt, body, 0)

    def wait_block(blk, buf, slot):
        def body(t, c):
            for k in range(top_k):
                row_copy(blk, buf, slot, t, k).wait()
            return c
        lax.fori_loop(0, tt, body, 0)

    def combine(buf):
        def per_token(t, c):
            base = (i * tt + t) * top_k
            acc = gates_ref[base] * buf[t]
            for k in range(1, top_k):
                acc = acc + gates_ref[base + k] * buf[k * tt + t]
            ysl_ref[t] = acc
            return c
        lax.fori_loop(0, tt, per_token, 0, unroll=4)
        y_ref[...] = pltpu.einshape("tsl->t(sl)", ysl_ref[...])

    even = i % 2 == 0

    @pl.when(i == 0)
    def _():
        start_block(0, buf0, 0)

    @pl.when((i + 1 < n_blocks) & even)
    def _():
        start_block(i + 1, buf1, 1)

    @pl.when((i + 1 < n_blocks) & jnp.logical_not(even))
    def _():
        start_block(i + 1, buf0, 0)

    @pl.when(even)
    def _():
        wait_block(i, buf0, 0)
        combine(buf0)

    @pl.when(jnp.logical_not(even))
    def _():
        wait_block(i, buf1, 1)
        combine(buf1)

    o_ref[...] = _layer_norm(DEEPNORM_ALPHA * x_ref[...] + gate_ref[0] * y_ref[...], g_ref[...], b_ref[...])


def combine_ln(pos, gates, out_rows, x, gate, ln_g, ln_b, seq_len, tt=128):
    n_tok, D = x.shape
    B = gate.shape[0]
    tt = min(tt, seq_len)
    P = out_rows.shape[1]
    kernel = functools.partial(_combine_ln_kernel, tt=tt, top_k=TOP_K_EXPERTS)
    return pl.pallas_call(
        kernel,
        grid_spec=pltpu.PrefetchScalarGridSpec(
            num_scalar_prefetch=2,
            grid=(n_tok // tt,),
            in_specs=[
                pl.BlockSpec(memory_space=pl.ANY),
                pl.BlockSpec((tt, D), lambda i, p, q: (i, 0)),
                pl.BlockSpec((1, 1, D), lambda i, p, q: (i * tt // seq_len, 0, 0)),
                pl.BlockSpec((1, D), lambda i, p, q: (0, 0)),
                pl.BlockSpec((1, D), lambda i, p, q: (0, 0)),
            ],
            out_specs=pl.BlockSpec((tt, D), lambda i, p, q: (i, 0)),
            scratch_shapes=[pltpu.VMEM((TOP_K_EXPERTS * tt, P, LANES), F32),
                            pltpu.VMEM((TOP_K_EXPERTS * tt, P, LANES), F32),
                            pltpu.VMEM((tt, P, LANES), F32),
                            pltpu.VMEM((tt, D), F32),
                            pltpu.SemaphoreType.DMA((2,))],
        ),
        out_shape=jax.ShapeDtypeStruct((n_tok, D), F32),
        compiler_params=_cparams(1),
        name="combine_ln",
    )(pos, gates, out_rows, x, gate.reshape(B, 1, D), ln_g.reshape(1, D), ln_b.reshape(1, D))


def route(logits, tm):
    n_tok = logits.shape[0]
    top_logit, top_e = lax.top_k(logits, TOP_K_EXPERTS)
    gates = jax.nn.softmax(top_logit, axis=-1)
    n_asg = n_tok * TOP_K_EXPERTS
    e_flat = top_e.reshape(n_asg)
    onehot = (e_flat[:, None] == jnp.arange(N_EXPERTS, dtype=e_flat.dtype)[None, :]).astype(jnp.int32)
    running = jnp.cumsum(onehot, axis=0)
    rank = jnp.take_along_axis(running, e_flat[:, None], axis=1)[:, 0] - 1
    counts = running[-1]
    padded = (counts + tm - 1) // tm * tm
    pad_end = jnp.cumsum(padded)
    pad_start = pad_end - padded
    pos = (pad_start[e_flat] + rank).astype(jnp.int32)
    n_blocks = -(-n_asg // tm) + N_EXPERTS
    blk_e = jnp.minimum(jnp.searchsorted(pad_end, jnp.arange(n_blocks) * tm, side='right'),
                        N_EXPERTS - 1).astype(jnp.int32)
    n_used = (pad_end[-1] // tm).astype(jnp.int32).reshape(1)
    return (gates, pos, blk_e, n_used, pad_end.astype(jnp.int32), counts.astype(jnp.int32),
            n_blocks * tm)


def moe_block(h2_slabs, logits, x1, gate_f, ln_g, ln_b, w_up, b_up, w_down, b_down, seq_len, tm=512):
    gates, pos, blk_e, n_used, pad_end, counts, n_rows = route(logits, tm)
    xg = scatter_rows(pos, pad_end, counts, h2_slabs, n_rows, tm)
    out_rows = moe_experts(xg, blk_e, n_used, w_up, b_up, w_down, b_down, tm)
    return combine_ln(pos, gates.reshape(-1), out_rows, x1, gate_f, ln_g, ln_b, seq_len)


def _layer(x, c, w_ada, b_ada, w_in, kv_norm_g, w_uk, w_uv, grp_norm_dsa, grp_norm_sb, w_out,
           ln1_g, ln1_b, w_router, b_router, w_up, b_up, w_down, b_down, ln2_g, ln2_b):
    B, S, D = x.shape
    n_tok = B * S
    n_dsa = w_uk.shape[0]
    w_qa = n_dsa * HEAD_DIM
    w_qidx = IDX_HEADS * IDX_DIM
    w_sb = grp_norm_sb.shape[0]
    n_sb = w_sb // HEAD_DIM

    mod = ada_modulation(c, w_ada, b_ada)
    shift_a, scale_a, gate_a, shift_f, scale_f, gate_f = jnp.split(mod, 6, axis=-1)

    o1 = w_qa
    o2 = o1 + KV_LATENT
    o3 = o2 + w_qidx
    o4 = o3 + IDX_DIM
    o5 = o4 + IDX_HEADS
    w_main = jnp.concatenate([w_in[:, o2:o3], w_in[:, :o1], w_in[:, o5:]], axis=1).astype(BF16)
    n_small = KV_LATENT + IDX_DIM + IDX_HEADS
    n_small_pad = KV_LATENT + IDX_DIM + LANES
    w_small = jnp.concatenate(
        [w_in[:, o1:o2], w_in[:, o3:o5], jnp.zeros((D, n_small_pad - n_small), F32)], axis=1).astype(BF16)

    h = modulate(x, scale_a, shift_a).reshape(n_tok, D)
    proj = matmul(h, w_main, BF16, tm=512, tn=512).reshape(B, S, w_main.shape[1])
    small = matmul(h, w_small, F32, tm=512, tn=n_small_pad)
    ckv, kidx = kv_prep(small, kv_norm_g)

    sb_w = SB_HEADS_PER_STEP * HEAD_DIM
    o_dsa = dsa_attention(
        qidx=(proj, 0), kidx=kidx.reshape(B, S, IDX_DIM),
        widx=(small.reshape(B, S, n_small_pad), (KV_LATENT + IDX_DIM) // LANES),
        qa=(proj, w_qidx // w_qa), ckv=ckv.reshape(B, S, KV_LATENT),
        w_uk=w_uk.astype(BF16), w_uv=w_uv.astype(BF16), g=grp_norm_dsa)
    sb0 = w_qidx + w_qa
    o_sb = stick_breaking_attention((proj, sb0 // sb_w), (proj, (sb0 + w_sb) // sb_w),
                                    (proj, (sb0 + 2 * w_sb) // sb_w), n_sb)

    x1, h2_slabs, logits = outproj_ln(
        o_dsa.reshape(n_tok, w_qa), o_sb.reshape(n_tok, w_sb), grp_norm_sb, w_out.astype(BF16),
        x.reshape(n_tok, D), gate_a, ln1_g, ln1_b, scale_f, shift_f, w_router, S)
    out = moe_block(h2_slabs, logits + b_router.astype(F32), x1, gate_f, ln2_g, ln2_b,
                    w_up, b_up, w_down, b_down, S)
    return out.reshape(B, S, D)


def kernel(x, c, w_ada, b_ada, w_in, kv_norm_g, w_uk, w_uv, grp_norm_dsa, grp_norm_sb, w_out, ln1_g, ln1_b, w_router, b_router, w_up, b_up, w_down, b_down, ln2_g, ln2_b):
    return _layer(x, c, w_ada[0], b_ada[0], w_in[0], kv_norm_g[0], w_uk[0], w_uv[0],
                  grp_norm_dsa[0], grp_norm_sb[0], w_out[0], ln1_g[0], ln1_b[0], w_router[0],
                  b_router[0], w_up[0], b_up[0], w_down[0], b_down[0], ln2_g[0], ln2_b[0])
```

```python
import functools
import math

import jax
import jax.numpy as jnp
from jax import lax
from jax.experimental import pallas as pl
from jax.experimental.pallas import tpu as pltpu

F32 = jnp.float32
BF16 = jnp.bfloat16

HEAD_DIM = 128
KV_LATENT = 512
IDX_HEADS = 32
IDX_DIM = 128
TOPK_MAX = 256
N_EXPERTS = 32
TOP_K_EXPERTS = 4
SWIGLU_ALPHA = 1.702
SWIGLU_LIMIT = 7.0
DEPTH = 1
DEEPNORM_ALPHA = (2 * DEPTH) ** 0.25
EPS = 1e-5
INDEX_SCALE = (IDX_HEADS * IDX_DIM) ** -0.5
ATTN_SCALE = HEAD_DIM ** -0.5
LOG2_E = math.log2(math.e)

LANES = 128
VMEM_LIMIT_BYTES = 56 * 1024 * 1024
VMEM_LIMIT_WIDE_ROWS_BYTES = 60 * 1024 * 1024

MASKED_LOGIT = -1e30
F32_EXP_UNDERFLOW = 104.0

NT_DIMS = (((1,), (1,)), ((), ()))


def _cparams(n_axes):
    return pltpu.CompilerParams(
        dimension_semantics=("arbitrary",) * n_axes, vmem_limit_bytes=VMEM_LIMIT_BYTES)


def _split_bf16(v):
    hi = v.astype(BF16)
    lo = (v - hi.astype(F32)).astype(BF16)
    return hi, lo


def _lane_tile(v, width):
    return jnp.concatenate([v] * (width // LANES), axis=1)


def _ada_kernel(c_ref, w_ref, b_ref, o_ref):
    c = c_ref[...]
    s = c * (1.0 / (1.0 + jnp.exp(-c)))
    s_hi, s_lo = _split_bf16(s)
    w_hi, w_lo = _split_bf16(w_ref[...])
    acc = jnp.dot(s_hi, w_hi, preferred_element_type=F32)
    acc += jnp.dot(s_hi, w_lo, preferred_element_type=F32)
    acc += jnp.dot(s_lo, w_hi, preferred_element_type=F32)
    o_ref[...] = acc + b_ref[...]


def ada_modulation(c, w_ada, b_ada, tn=512):
    B, D = c.shape
    N = w_ada.shape[1]
    rows = 8
    c_pad = jnp.zeros((rows, D), F32).at[:B].set(c)
    out = pl.pallas_call(
        _ada_kernel,
        grid=(N // tn,),
        in_specs=[pl.BlockSpec((rows, D), lambda j: (0, 0)),
                  pl.BlockSpec((D, tn), lambda j: (0, j)),
                  pl.BlockSpec((1, tn), lambda j: (0, j))],
        out_specs=pl.BlockSpec((rows, tn), lambda j: (0, j)),
        out_shape=jax.ShapeDtypeStruct((rows, N), F32),
        compiler_params=_cparams(1),
        name="ada_modulation",
    )(c_pad, w_ada, b_ada.reshape(1, N))
    return out[:B]


def _inproj_kernel(x_ref, sc_ref, sh_ref, wm_ref, ws_ref, g_ref, proj_ref, ckv_ref, kidx_ref, widx_ref,
                   h_ref, *, n_main):
    j = pl.program_id(1)

    @pl.when(j == 0)
    def _():
        h_ref[...] = (x_ref[...] * (1.0 + sc_ref[0]) + sh_ref[0]).astype(BF16)

    @pl.when(j < n_main)
    def _():
        proj_ref[...] = jnp.dot(h_ref[...], wm_ref[...], preferred_element_type=F32).astype(BF16)

    @pl.when(j == n_main)
    def _():
        s = jnp.dot(h_ref[...], ws_ref[...], preferred_element_type=F32)
        lat = s[:, :KV_LATENT]
        ms = jnp.mean(lat * lat, axis=-1, keepdims=True)
        ckv_ref[...] = (lat * lax.rsqrt(ms + EPS) * g_ref[...]).astype(BF16)
        kidx_ref[...] = s[:, KV_LATENT:KV_LATENT + IDX_DIM].astype(BF16)
        widx_ref[...] = s[:, KV_LATENT + IDX_DIM:]


def input_projection(x, scale, shift, w_main, w_small, kv_norm_g, seq_len, tm=512, tn=512):
    M, D = x.shape
    B = scale.shape[0]
    n_mainw = w_main.shape[1]
    tm, tn = min(tm, seq_len), min(tn, n_mainw)
    n_main = n_mainw // tn
    n_small = w_small.shape[1]

    def per_batch(i, j):
        return (i * tm // seq_len, 0, 0)

    row = lambda i, j: (i, 0)
    main_col = lambda i, j: (0, jnp.minimum(j, n_main - 1))
    return pl.pallas_call(
        functools.partial(_inproj_kernel, n_main=n_main),
        grid=(M // tm, n_main + 1),
        in_specs=[
            pl.BlockSpec((tm, D), row),
            pl.BlockSpec((1, 1, D), per_batch),
            pl.BlockSpec((1, 1, D), per_batch),
            pl.BlockSpec((D, tn), main_col),
            pl.BlockSpec((D, n_small), lambda i, j: (0, 0), pipeline_mode=pl.Buffered(1)),
            pl.BlockSpec((1, KV_LATENT), lambda i, j: (0, 0)),
        ],
        out_specs=[
            pl.BlockSpec((tm, tn), lambda i, j: (i, jnp.minimum(j, n_main - 1))),
            pl.BlockSpec((tm, KV_LATENT), row),
            pl.BlockSpec((tm, IDX_DIM), row),
            pl.BlockSpec((tm, LANES), row),
        ],
        out_shape=[jax.ShapeDtypeStruct((M, n_mainw), BF16), jax.ShapeDtypeStruct((M, KV_LATENT), BF16),
                   jax.ShapeDtypeStruct((M, IDX_DIM), BF16), jax.ShapeDtypeStruct((M, LANES), F32)],
        scratch_shapes=[pltpu.VMEM((tm, D), BF16)],
        compiler_params=_cparams(2),
        name="input_projection",
    )(x, scale.reshape(B, 1, D), shift.reshape(B, 1, D), w_main, w_small, kv_norm_g.reshape(1, KV_LATENT))


def _sortable_key(v):
    bits = pltpu.bitcast(v + 0.0, jnp.int32)
    return bits ^ (lax.shift_right_arithmetic(bits, 31) & jnp.int32(0x7FFFFFFF))


def _dsa_kernel(qidx_ref, kidx_ref, widx_ref, qa_ref, ckv_ref, wuk_ref, wuv_ref, g_ref, o_ref,
                key_ref, wb_ref, q2_ref, qlat_ref, lg_ref, bias_ref, p_ref, acc_ref, m_ref, l_ref,
                alpha_ref, slope_ref, tie_end_ref, obuf_ref, *, tq, tk_score, tk, k_sel, n_heads, n_idx):
    s_len = key_ref.shape[1]
    pos_bits = max(1, (s_len - 1).bit_length())
    i = pl.program_id(1)
    t0 = i * tq
    n_att = (t0 + tq - 1) // tk + 1
    n_score = n_att * (tk // tk_score)

    w = widx_ref[0]
    for h in range(n_idx):
        wb_ref[h] = jnp.broadcast_to(w[:, h:h + 1], (tq, LANES))
        q2_ref[h * tq:(h + 1) * tq, :] = qidx_ref[0, :, h * IDX_DIM:(h + 1) * IDX_DIM]
    t_ids_s = t0 + lax.broadcasted_iota(jnp.int32, (tq, tk_score), 0)
    lane_ids_s = lax.broadcasted_iota(jnp.int32, (tq, tk_score), 1)

    def score_chunk(j, carry):
        s0 = pl.multiple_of(j * tk_score, tk_score)
        kb = kidx_ref[0, pl.ds(s0, tk_score), :]
        lg = lax.dot_general(q2_ref[...], kb, NT_DIMS, preferred_element_type=F32)
        r = jnp.maximum(lg, 0.0).reshape(n_idx, tq, tk_score)
        parts = [jnp.sum(r[:, :, c * LANES:(c + 1) * LANES] * wb_ref[...], axis=0)
                 for c in range(tk_score // LANES)]
        sc = jnp.concatenate(parts, axis=1) * INDEX_SCALE
        sc = jnp.where(s0 + lane_ids_s <= t_ids_s, sc, -jnp.inf)
        key_ref[:, pl.ds(s0, tk_score)] = _sortable_key(sc)
        return carry

    lax.fori_loop(0, n_score, score_chunk, 0)

    lane_ids = lax.broadcasted_iota(jnp.int32, (tq, tk), 1)

    def count_where(preds):
        def chunk(j, cnts):
            s0 = pl.multiple_of(j * tk, tk)
            keys = key_ref[:, pl.ds(s0, tk)]
            s_ids = s0 + lane_ids
            hits = [jnp.where(pred(keys, s_ids), 1.0, 0.0) for pred in preds]
            return tuple(cnt + sum(hit[:, c * LANES:(c + 1) * LANES] for c in range(tk // LANES))
                         for cnt, hit in zip(cnts, hits))
        zero = jnp.zeros((tq, LANES), F32)
        cnts = lax.fori_loop(0, n_att, chunk, (zero,) * len(preds))
        return [jnp.sum(cnt, axis=1, keepdims=True) for cnt in cnts]

    def bisect(it, prefix):
        cand = prefix + lax.shift_left(jnp.int32(1), 31 - it)
        cand_t = _lane_tile(cand, tk)
        (total,) = count_where([lambda keys, s_ids: keys >= cand_t])
        return jnp.where(total >= float(k_sel), cand, prefix)

    thr = lax.fori_loop(0, 32, bisect, jnp.full((tq, LANES), -2 ** 31, jnp.int32))
    thr_t = _lane_tile(thr, tk)

    n_above, n_at_least = count_where([lambda keys, s_ids: keys > thr_t,
                                       lambda keys, s_ids: keys >= thr_t])
    need = float(k_sel) - n_above
    tie_end_ref[...] = jnp.full((tq, LANES), s_len, jnp.int32)

    @pl.when(jnp.max(n_at_least) > float(k_sel))
    def _():
        def grow(it, p):
            cand = p | lax.shift_left(jnp.int32(1), pos_bits - 1 - it)
            cand_t = _lane_tile(cand, tk)
            (below,) = count_where([lambda keys, s_ids: (keys == thr_t) & (s_ids < cand_t)])
            return jnp.where(below < need, cand, p)
        tie_end_ref[...] = lax.fori_loop(0, pos_bits, grow, jnp.zeros((tq, LANES), jnp.int32))

    for h in range(n_heads):
        ql = jnp.dot(qa_ref[0, :, h * HEAD_DIM:(h + 1) * HEAD_DIM], wuk_ref[h], preferred_element_type=F32)
        qlat_ref[h * tq:(h + 1) * tq, :] = (ql * (ATTN_SCALE * LOG2_E)).astype(BF16)
    head_no = lax.broadcasted_iota(jnp.int32, slope_ref.shape, 0).astype(F32)
    slope_ref[...] = jnp.exp2(-8.0 * (head_no + 1.0) / n_heads) * LOG2_E
    m_ref[...] = jnp.full(m_ref.shape, MASKED_LOGIT, F32)
    l_ref[...] = jnp.zeros(l_ref.shape, F32)
    acc_ref[...] = jnp.zeros(acc_ref.shape, F32)
    t_ids = t0 + lax.broadcasted_iota(jnp.int32, (tq, tk), 0)
    col_ids = lax.broadcasted_iota(jnp.int32, (1, tk), 1)

    def attend_chunk(j, carry):
        s0 = pl.multiple_of(j * tk, tk)
        cb = ckv_ref[0, pl.ds(s0, tk), :]
        lg_ref[...] = lax.dot_general(qlat_ref[...], cb, NT_DIMS, preferred_element_type=F32)
        keys = key_ref[:, pl.ds(s0, tk)]
        s_ids = s0 + lane_ids
        sel = (keys > thr_t) | ((keys == thr_t) & (s_ids <= _lane_tile(tie_end_ref[...], tk)))
        sel = sel & (s_ids <= t_ids)
        bias_ref[...] = jnp.where(sel, 0.0, MASKED_LOGIT)
        rel = (s0 - t0 + col_ids).astype(F32)

        def head(h, c2):
            rows = pl.ds(pl.multiple_of(h * tq, tq), tq)
            x = lg_ref[rows, :] + _lane_tile(slope_ref[h], tk) * rel + bias_ref[...]
            m_old = m_ref[rows, :]
            m_new = jnp.maximum(m_old, jnp.max(x, axis=1, keepdims=True))
            alpha = jnp.exp2(m_old - m_new)
            p = jnp.exp2(x - _lane_tile(m_new, tk))
            l_ref[rows, :] = alpha * l_ref[rows, :] + jnp.sum(p, axis=1, keepdims=True)
            m_ref[rows, :] = m_new
            alpha_ref[rows, :] = alpha
            p_ref[rows, :] = p.astype(BF16)
            return c2

        lax.fori_loop(0, n_heads, head, 0, unroll=True)
        lat = acc_ref.shape[1]
        acc_ref[...] = (acc_ref[...] * _lane_tile(alpha_ref[...], lat)
                        + jnp.dot(p_ref[...], cb, preferred_element_type=F32))
        return carry

    lax.fori_loop(0, n_att, attend_chunk, 0)

    ss = jnp.zeros((tq, 1), F32)
    for h in range(n_heads):
        rows = slice(h * tq, (h + 1) * tq)
        inv_l = _lane_tile(1.0 / l_ref[rows, :], acc_ref.shape[1])
        oh = jnp.dot((acc_ref[rows, :] * inv_l).astype(BF16), wuv_ref[h],
                     preferred_element_type=F32)
        obuf_ref[:, h * HEAD_DIM:(h + 1) * HEAD_DIM] = oh
        ss = ss + jnp.sum(oh * oh, axis=1, keepdims=True)
    inv = lax.rsqrt(ss * (1.0 / (n_heads * HEAD_DIM)) + EPS)
    o_ref[0] = (obuf_ref[...] * inv * g_ref[...]).astype(o_ref.dtype)


def dsa_attention(qidx, kidx, widx, qa, ckv, w_uk, w_uv, g, tq=128, tk_score=512, tk=512):
    qidx_arr, qidx_blk = qidx
    qa_arr, qa_blk = qa
    widx_arr, widx_blk = widx
    B, S, lat = ckv.shape
    n_heads = w_uk.shape[0]
    n_idx = IDX_HEADS
    tq, tk = min(tq, S), min(tk, S)
    tk_score = min(tk_score, tk)
    k_sel = min(TOPK_MAX, S // 4)
    width = n_heads * HEAD_DIM
    kernel = functools.partial(_dsa_kernel, tq=tq, tk_score=tk_score, tk=tk, k_sel=k_sel,
                               n_heads=n_heads, n_idx=n_idx)
    once = pl.Buffered(1)
    return pl.pallas_call(
        kernel,
        grid=(B, S // tq),
        in_specs=[
            pl.BlockSpec((1, tq, n_idx * IDX_DIM), lambda b, i: (b, i, qidx_blk)),
            pl.BlockSpec((1, S, IDX_DIM), lambda b, i: (b, 0, 0), pipeline_mode=once),
            pl.BlockSpec((1, tq, LANES), lambda b, i: (b, i, widx_blk)),
            pl.BlockSpec((1, tq, width), lambda b, i: (b, i, qa_blk)),
            pl.BlockSpec((1, S, lat), lambda b, i: (b, 0, 0), pipeline_mode=once),
            pl.BlockSpec((n_heads, HEAD_DIM, lat), lambda b, i: (0, 0, 0), pipeline_mode=once),
            pl.BlockSpec((n_heads, lat, HEAD_DIM), lambda b, i: (0, 0, 0), pipeline_mode=once),
            pl.BlockSpec((1, width), lambda b, i: (0, 0)),
        ],
        out_specs=pl.BlockSpec((1, tq, width), lambda b, i: (b, i, 0)),
        out_shape=jax.ShapeDtypeStruct((B, S, width), BF16),
        scratch_shapes=[
            pltpu.VMEM((tq, S), jnp.int32),
            pltpu.VMEM((n_idx, tq, LANES), F32),
            pltpu.VMEM((n_idx * tq, IDX_DIM), BF16),
            pltpu.VMEM((n_heads * tq, lat), BF16),
            pltpu.VMEM((n_heads * tq, tk), F32),
            pltpu.VMEM((tq, tk), F32),
            pltpu.VMEM((n_heads * tq, tk), BF16),
            pltpu.VMEM((n_heads * tq, lat), F32),
            pltpu.VMEM((n_heads * tq, LANES), F32),
            pltpu.VMEM((n_heads * tq, LANES), F32),
            pltpu.VMEM((n_heads * tq, LANES), F32),
            pltpu.VMEM((n_heads, 1, LANES), F32),
            pltpu.VMEM((tq, LANES), jnp.int32),
            pltpu.VMEM((tq, width), F32),
        ],
        compiler_params=_cparams(2),
        name="dsa_attention",
    )(qidx_arr, kidx, widx_arr, qa_arr, ckv, w_uk, w_uv, g.reshape(1, width))


def _sb_kernel(q_ref, k_ref, v_ref, o_ref, acc_ref, carry_ref, *, tq, hb):
    i = pl.program_id(2)
    t0 = i * tq
    row = lax.broadcasted_iota(jnp.int32, (tq, tq), 0)
    col = lax.broadcasted_iota(jnp.int32, (tq, tq), 1)
    suffix_ones = jnp.where(row > col, 1.0, 0.0).astype(BF16)
    acc_ref[...] = jnp.zeros(acc_ref.shape, F32)
    carry_ref[...] = jnp.zeros(carry_ref.shape, F32)

    def keep_going(state):
        j, carry_min = state
        return (j >= 0) & (carry_min <= F32_EXP_UNDERFLOW)

    def block(state):
        j, _ = state
        s0 = pl.multiple_of(j * tq, tq)
        strict = (s0 + col) < (t0 + row)
        carry_min = None
        for h in range(hb):
            cols = slice(h * HEAD_DIM, (h + 1) * HEAD_DIM)
            kb = k_ref[0, pl.ds(s0, tq), cols]
            vb = v_ref[0, pl.ds(s0, tq), cols]
            z = lax.dot_general(q_ref[0, :, cols], kb, NT_DIMS, preferred_element_type=F32) * ATTN_SCALE
            sp = jnp.maximum(z, 0.0) + jnp.log(1.0 + jnp.exp(-jnp.abs(z)))
            u = jnp.where(strict, sp, 0.0)
            u_hi, u_lo = _split_bf16(u)
            within = (jnp.dot(u_hi, suffix_ones, preferred_element_type=F32)
                      + jnp.dot(u_lo, suffix_ones, preferred_element_type=F32))
            carry = carry_ref[:, cols]
            a = jnp.where(strict, jnp.exp(z - sp - (_lane_tile(carry, tq) + within)), 0.0)
            acc_ref[:, cols] += jnp.dot(a.astype(BF16), vb, preferred_element_type=F32)
            carry = carry + jnp.sum(u, axis=1, keepdims=True)
            carry_ref[:, cols] = carry
            head_min = jnp.min(carry)
            carry_min = head_min if carry_min is None else jnp.minimum(carry_min, head_min)
        return j - 1, carry_min

    lax.while_loop(keep_going, block, (i, jnp.float32(0.0)))
    o_ref[0] = acc_ref[...]


SB_HEADS_PER_STEP = 8


def stick_breaking_attention(q, k, v, n_heads, tq=256, hb=SB_HEADS_PER_STEP):
    (q_arr, q_blk), (k_arr, k_blk), (v_arr, v_blk) = q, k, v
    B, S, _ = q_arr.shape
    tq = min(tq, S)
    wb = hb * HEAD_DIM
    once = pl.Buffered(1)
    return pl.pallas_call(
        functools.partial(_sb_kernel, tq=tq, hb=hb),
        grid=(B, n_heads // hb, S // tq),
        in_specs=[pl.BlockSpec((1, tq, wb), lambda b, h, i: (b, i, q_blk + h)),
                  pl.BlockSpec((1, S, wb), lambda b, h, i: (b, 0, k_blk + h), pipeline_mode=once),
                  pl.BlockSpec((1, S, wb), lambda b, h, i: (b, 0, v_blk + h), pipeline_mode=once)],
        out_specs=pl.BlockSpec((1, tq, wb), lambda b, h, i: (b, i, h)),
        out_shape=jax.ShapeDtypeStruct((B, S, n_heads * HEAD_DIM), F32),
        scratch_shapes=[pltpu.VMEM((tq, wb), F32), pltpu.VMEM((tq, wb), F32)],
        compiler_params=_cparams(3),
        name="stick_breaking_attention",
    )(q_arr, k_arr, v_arr)


def _layer_norm(v, g, b):
    mu = jnp.mean(v, axis=-1, keepdims=True)
    d = v - mu
    var = jnp.mean(d * d, axis=-1, keepdims=True)
    return d * lax.rsqrt(var + EPS) * g + b


def _outproj_ln_kernel(od_ref, os_ref, gsb_ref, w_ref, x_ref, gate_ref, g_ref, b_ref, sc_ref, sh_ref,
                       wr_ref, x1_ref, h2_ref, lg_ref, a_ref, *, tn):
    j = pl.program_id(1)
    wd = od_ref.shape[1]

    @pl.when(j == 0)
    def _():
        a_ref[:, :wd] = od_ref[...]
        o = os_ref[...]
        ms = jnp.mean(o * o, axis=-1, keepdims=True)
        a_ref[:, wd:] = (o * lax.rsqrt(ms + EPS) * gsb_ref[...]).astype(BF16)

    x1_ref[:, pl.ds(pl.multiple_of(j * tn, tn), tn)] = jnp.dot(
        a_ref[...], w_ref[...], preferred_element_type=F32)

    @pl.when(j == pl.num_programs(1) - 1)
    def _():
        y = _layer_norm(DEEPNORM_ALPHA * x_ref[...] + gate_ref[0] * x1_ref[...], g_ref[...], b_ref[...])
        x1_ref[...] = y
        h2 = (y * (1.0 + sc_ref[0]) + sh_ref[0]).astype(BF16)
        lg_ref[...] = jnp.dot(h2, wr_ref[...], preferred_element_type=F32)
        h2_ref[...] = pltpu.einshape("t(sl)->tsl", h2, s=h2_ref.shape[1])


def outproj_ln(o_dsa, o_sb, g_sb, w_out, x, gate, ln_g, ln_b, scale, shift, w_router, seq_len,
               tm=512, tn=256):
    M, D = x.shape
    wd, ws = o_dsa.shape[1], o_sb.shape[1]
    B = gate.shape[0]
    tm, tn = min(tm, seq_len), min(tn, D)
    n_exp = w_router.shape[1]
    wr = jnp.zeros((D, LANES), BF16).at[:, :n_exp].set(w_router.astype(BF16))

    def per_batch(i, j):
        return (i * tm // seq_len, 0, 0)

    row = lambda i, j: (i, 0)
    shared = lambda i, j: (0, 0)
    once = pl.Buffered(1)
    x1, h2_slabs, logits = pl.pallas_call(
        functools.partial(_outproj_ln_kernel, tn=tn),
        grid=(M // tm, D // tn),
        in_specs=[
            pl.BlockSpec((tm, wd), row),
            pl.BlockSpec((tm, ws), row, pipeline_mode=once),
            pl.BlockSpec((1, ws), shared),
            pl.BlockSpec((wd + ws, tn), lambda i, j: (0, j)),
            pl.BlockSpec((tm, D), row, pipeline_mode=once),
            pl.BlockSpec((1, 1, D), per_batch),
            pl.BlockSpec((1, D), shared),
            pl.BlockSpec((1, D), shared),
            pl.BlockSpec((1, 1, D), per_batch),
            pl.BlockSpec((1, 1, D), per_batch),
            pl.BlockSpec((D, LANES), shared),
        ],
        out_specs=[pl.BlockSpec((tm, D), row),
                   pl.BlockSpec((tm, D // LANES, LANES), lambda i, j: (i, 0, 0)),
                   pl.BlockSpec((tm, LANES), row)],
        out_shape=[jax.ShapeDtypeStruct((M, D), F32),
                   jax.ShapeDtypeStruct((M, D // LANES, LANES), BF16),
                   jax.ShapeDtypeStruct((M, LANES), F32)],
        scratch_shapes=[pltpu.VMEM((tm, wd + ws), BF16)],
        compiler_params=pltpu.CompilerParams(
            dimension_semantics=("arbitrary", "arbitrary"), vmem_limit_bytes=VMEM_LIMIT_WIDE_ROWS_BYTES),
        name="outproj_ln",
    )(o_dsa, o_sb, g_sb.reshape(1, ws), w_out, x, gate.reshape(B, 1, D), ln_g.reshape(1, D),
      ln_b.reshape(1, D), scale.reshape(B, 1, D), shift.reshape(B, 1, D), wr)
    return x1, h2_slabs, logits[:, :n_exp]


def _scatter_rows_kernel(pos_ref, pad_end_ref, cnt_ref, h_ref, xg_ref, zero_ref, sem, *, tm, tt, top_k,
                         n_exp):
    i = pl.program_id(0)

    def fill_copy(e):
        start = pl.multiple_of(pad_end_ref[e] - tm, tm)
        return pltpu.make_async_copy(zero_ref, xg_ref.at[pl.ds(start, tm)], sem.at[0])

    @pl.when(i == 0)
    def _():
        zero_ref[...] = jnp.zeros(zero_ref.shape, zero_ref.dtype)

        def start_fill(e, c):
            @pl.when(cnt_ref[e] % tm != 0)
            def _():
                fill_copy(e).start()
            return c

        def wait_fill(e, c):
            @pl.when(cnt_ref[e] % tm != 0)
            def _():
                fill_copy(e).wait()
            return c

        def tail_copy(r):
            return pltpu.make_async_copy(
                zero_ref, xg_ref.at[pl.ds(pl.multiple_of(r * tm, tm), tm)], sem.at[0])

        def start_tail(r, c):
            tail_copy(r).start()
            return c

        def wait_tail(r, c):
            tail_copy(r).wait()
            return c

        n_used = pad_end_ref[n_exp - 1] // tm
        n_blocks = xg_ref.shape[0] // tm
        lax.fori_loop(0, n_exp, start_fill, 0)
        lax.fori_loop(n_used, n_blocks, start_tail, 0)
        lax.fori_loop(0, n_exp, wait_fill, 0)
        lax.fori_loop(n_used, n_blocks, wait_tail, 0)

    base = i * tt

    def row_copy(t, k):
        return pltpu.make_async_copy(h_ref.at[t], xg_ref.at[pos_ref[(base + t) * top_k + k]], sem.at[1])

    def start_rows(t, c):
        for k in range(top_k):
            row_copy(t, k).start(priority=k % 2)
        return c

    def wait_rows(t, c):
        for k in range(top_k):
            row_copy(t, k).wait()
        return c

    lax.fori_loop(0, tt, start_rows, 0)
    lax.fori_loop(0, tt, wait_rows, 0)


def scatter_rows(pos, pad_end, counts, h_slabs, n_rows, tm, tt=256):
    n_tok, P, _ = h_slabs.shape
    dtype = h_slabs.dtype
    tt = min(tt, n_tok)
    kernel = functools.partial(_scatter_rows_kernel, tm=tm, tt=tt, top_k=TOP_K_EXPERTS, n_exp=N_EXPERTS)
    return pl.pallas_call(
        kernel,
        grid_spec=pltpu.PrefetchScalarGridSpec(
            num_scalar_prefetch=3,
            grid=(n_tok // tt,),
            in_specs=[pl.BlockSpec((tt, P, LANES), lambda i, p, e, c: (i, 0, 0))],
            out_specs=pl.BlockSpec(memory_space=pl.ANY),
            scratch_shapes=[pltpu.VMEM((tm, P, LANES), dtype), pltpu.SemaphoreType.DMA((2,))],
        ),
        out_shape=jax.ShapeDtypeStruct((n_rows, P, LANES), dtype),
        compiler_params=_cparams(1),
        name="scatter_rows",
    )(pos, pad_end, counts, h_slabs)


def _expert_changed(blk_e_ref, r):
    return (r == 0) | (blk_e_ref[r] != blk_e_ref[jnp.maximum(r - 1, 0)])


def _moe_up_kernel(blk_e_ref, n_used_ref, x_ref, wg_ref, wl_ref, bg_ref, bl_ref, o_ref, wg_bf, wl_bf):
    r = pl.program_id(1)
    used = r < n_used_ref[0]

    @pl.when(used & _expert_changed(blk_e_ref, r))
    def _():
        wg_bf[...] = wg_ref[0].astype(BF16)
        wl_bf[...] = wl_ref[0].astype(BF16)

    @pl.when(used)
    def _():
        x = pltpu.einshape("tsl->t(sl)", x_ref[...])
        glu = jnp.dot(x, wg_bf[...], preferred_element_type=F32) + bg_ref[0]
        lin = jnp.dot(x, wl_bf[...], preferred_element_type=F32) + bl_ref[0]
        glu = jnp.minimum(glu, SWIGLU_LIMIT)
        lin = jnp.clip(lin, -SWIGLU_LIMIT, SWIGLU_LIMIT)
        act = glu * (1.0 / (1.0 + jnp.exp(-SWIGLU_ALPHA * glu))) * (lin + 1.0)
        o_ref[...] = act.astype(o_ref.dtype)

    @pl.when(jnp.logical_not(used))
    def _():
        o_ref[...] = jnp.zeros(o_ref.shape, o_ref.dtype)


def _moe_down_kernel(blk_e_ref, n_used_ref, a_ref, wd_ref, bd_ref, o_ref, wd_bf):
    r = pl.program_id(1)
    used = r < n_used_ref[0]

    @pl.when(used & _expert_changed(blk_e_ref, r))
    def _():
        wd_bf[...] = wd_ref[0].astype(BF16)

    @pl.when(used)
    def _():
        out = jnp.dot(a_ref[...], wd_bf[...], preferred_element_type=F32) + bd_ref[0]
        o_ref[...] = pltpu.einshape("t(sl)->tsl", out, s=o_ref.shape[1])

    @pl.when(jnp.logical_not(used))
    def _():
        o_ref[...] = jnp.zeros(o_ref.shape, o_ref.dtype)


def moe_experts(xg, blk_e, n_used, w_up, b_up, w_down, b_down, tm, tn_up=512, tn_down=2048):
    R, P, _ = xg.shape
    D = P * LANES
    n_exp, d_exp = w_down.shape[0], w_down.shape[1]
    tn_up, tn_down = min(tn_up, d_exp), min(tn_down, D)
    n_up = d_exp // tn_up

    def row(r, n):
        return jnp.minimum(r, n[0] - 1)

    def expert(r, e, n):
        return e[row(r, n)]

    b_up3 = b_up.reshape(n_exp, 1, 2 * d_exp)
    act = pl.pallas_call(
        _moe_up_kernel,
        grid_spec=pltpu.PrefetchScalarGridSpec(
            num_scalar_prefetch=2,
            grid=(n_up, R // tm),
            in_specs=[
                pl.BlockSpec((tm, P, LANES), lambda c, r, e, n: (row(r, n), 0, 0)),
                pl.BlockSpec((1, D, tn_up), lambda c, r, e, n: (expert(r, e, n), 0, c)),
                pl.BlockSpec((1, D, tn_up), lambda c, r, e, n: (expert(r, e, n), 0, n_up + c)),
                pl.BlockSpec((1, 1, tn_up), lambda c, r, e, n: (expert(r, e, n), 0, c)),
                pl.BlockSpec((1, 1, tn_up), lambda c, r, e, n: (expert(r, e, n), 0, n_up + c)),
            ],
            out_specs=pl.BlockSpec((tm, tn_up), lambda c, r, e, n: (r, c)),
            scratch_shapes=[pltpu.VMEM((D, tn_up), BF16), pltpu.VMEM((D, tn_up), BF16)],
        ),
        out_shape=jax.ShapeDtypeStruct((R, d_exp), BF16),
        compiler_params=_cparams(2),
        name="moe_up",
    )(blk_e, n_used, xg, w_up, w_up, b_up3, b_up3)
    return pl.pallas_call(
        _moe_down_kernel,
        grid_spec=pltpu.PrefetchScalarGridSpec(
            num_scalar_prefetch=2,
            grid=(D // tn_down, R // tm),
            in_specs=[
                pl.BlockSpec((tm, d_exp), lambda c, r, e, n: (row(r, n), 0)),
                pl.BlockSpec((1, d_exp, tn_down), lambda c, r, e, n: (expert(r, e, n), 0, c)),
                pl.BlockSpec((1, 1, tn_down), lambda c, r, e, n: (expert(r, e, n), 0, c)),
            ],
            out_specs=pl.BlockSpec((tm, tn_down // LANES, LANES), lambda c, r, e, n: (r, c, 0)),
            scratch_shapes=[pltpu.VMEM((d_exp, tn_down), BF16)],
        ),
        out_shape=jax.ShapeDtypeStruct((R, D // LANES, LANES), F32),
        compiler_params=_cparams(2),
        name="moe_down",
    )(blk_e, n_used, act, w_down, b_down.reshape(n_exp, 1, D))


def _combine_ln_kernel(pos_ref, gates_ref, rows_ref, x_ref, gate_ref, g_ref, b_ref, o_ref,
                       buf0, buf1, ysl_ref, y_ref, sem, *, tt, top_k):
    i = pl.program_id(0)
    n_blocks = pl.num_programs(0)

    def row_copy(blk, buf, slot, t, k):
        src = rows_ref.at[pos_ref[(blk * tt + t) * top_k + k]]
        return pltpu.make_async_copy(src, buf.at[k * tt + t], sem.at[slot])

    def start_block(blk, buf, slot):
        def body(t, c):
            for k in range(top_k):
                row_copy(blk, buf, slot, t, k).start(priority=k % 2)
            return c
        lax.fori_loop(0, tt, body, 0)

    def wait_block(blk, buf, slot):
        def body(t, c):
            for k in range(top_k):
                row_copy(blk, buf, slot, t, k).wait()
            return c
        lax.fori_loop(0, tt, body, 0)

    def combine(buf):
        def per_token(t, c):
            base = (i * tt + t) * top_k
            acc = gates_ref[base] * buf[t]
            for k in range(1, top_k):
                acc = acc + gates_ref[base + k] * buf[k * tt + t]
            ysl_ref[t] = acc
            return c
        lax.fori_loop(0, tt, per_token, 0, unroll=4)
        y_ref[...] = pltpu.einshape("tsl->t(sl)", ysl_ref[...])

    even = i % 2 == 0

    @pl.when(i == 0)
    def _():
        start_block(0, buf0, 0)

    @pl.when((i + 1 < n_blocks) & even)
    def _():
        start_block(i + 1, buf1, 1)

    @pl.when((i + 1 < n_blocks) & jnp.logical_not(even))
    def _():
        start_block(i + 1, buf0, 0)

    @pl.when(even)
    def _():
        wait_block(i, buf0, 0)
        combine(buf0)

    @pl.when(jnp.logical_not(even))
    def _():
        wait_block(i, buf1, 1)
        combine(buf1)

    o_ref[...] = _layer_norm(DEEPNORM_ALPHA * x_ref[...] + gate_ref[0] * y_ref[...], g_ref[...], b_ref[...])


def combine_ln(pos, gates, out_rows, x, gate, ln_g, ln_b, seq_len, tt=128):
    n_tok, D = x.shape
    B = gate.shape[0]
    tt = min(tt, seq_len)
    P = out_rows.shape[1]
    kernel = functools.partial(_combine_ln_kernel, tt=tt, top_k=TOP_K_EXPERTS)
    return pl.pallas_call(
        kernel,
        grid_spec=pltpu.PrefetchScalarGridSpec(
            num_scalar_prefetch=2,
            grid=(n_tok // tt,),
            in_specs=[
                pl.BlockSpec(memory_space=pl.ANY),
                pl.BlockSpec((tt, D), lambda i, p, q: (i, 0)),
                pl.BlockSpec((1, 1, D), lambda i, p, q: (i * tt // seq_len, 0, 0)),
                pl.BlockSpec((1, D), lambda i, p, q: (0, 0)),
                pl.BlockSpec((1, D), lambda i, p, q: (0, 0)),
            ],
            out_specs=pl.BlockSpec((tt, D), lambda i, p, q: (i, 0)),
            scratch_shapes=[pltpu.VMEM((TOP_K_EXPERTS * tt, P, LANES), F32),
                            pltpu.VMEM((TOP_K_EXPERTS * tt, P, LANES), F32),
                            pltpu.VMEM((tt, P, LANES), F32),
                            pltpu.VMEM((tt, D), F32),
                            pltpu.SemaphoreType.DMA((2,))],
        ),
        out_shape=jax.ShapeDtypeStruct((n_tok, D), F32),
        compiler_params=_cparams(1),
        name="combine_ln",
    )(pos, gates, out_rows, x, gate.reshape(B, 1, D), ln_g.reshape(1, D), ln_b.reshape(1, D))


def route(logits, tm):
    n_tok = logits.shape[0]
    top_logit, top_e = lax.top_k(logits, TOP_K_EXPERTS)
    gates = jax.nn.softmax(top_logit, axis=-1)
    n_asg = n_tok * TOP_K_EXPERTS
    e_flat = top_e.reshape(n_asg)
    onehot = (e_flat[:, None] == jnp.arange(N_EXPERTS, dtype=e_flat.dtype)[None, :]).astype(jnp.int32)
    assert n_asg % LANES == 0
    blocks = onehot.reshape(n_asg // LANES, LANES, N_EXPERTS).astype(F32)
    lower = jnp.tril(jnp.ones((LANES, LANES), F32))
    within = jnp.einsum('ij,bjk->bik', lower, blocks)
    block_tot = within[:, -1, :]
    before = jnp.cumsum(block_tot, axis=0) - block_tot
    running = (within + before[:, None, :]).reshape(n_asg, N_EXPERTS).astype(jnp.int32)
    rank = jnp.take_along_axis(running, e_flat[:, None], axis=1)[:, 0] - 1
    counts = running[-1]
    padded = (counts + tm - 1) // tm * tm
    pad_end = jnp.cumsum(padded)
    pad_start = pad_end - padded
    pos = (pad_start[e_flat] + rank).astype(jnp.int32)
    n_blocks = -(-n_asg // tm) + N_EXPERTS
    blk_e = jnp.minimum(jnp.searchsorted(pad_end, jnp.arange(n_blocks) * tm, side='right'),
                        N_EXPERTS - 1).astype(jnp.int32)
    n_used = (pad_end[-1] // tm).astype(jnp.int32).reshape(1)
    return (gates, pos, blk_e, n_used, pad_end.astype(jnp.int32), counts.astype(jnp.int32),
            n_blocks * tm)


def moe_block(h2_slabs, logits, x1, gate_f, ln_g, ln_b, w_up, b_up, w_down, b_down, seq_len, tm=512):
    gates, pos, blk_e, n_used, pad_end, counts, n_rows = route(logits, tm)
    xg = scatter_rows(pos, pad_end, counts, h2_slabs, n_rows, tm)
    out_rows = moe_experts(xg, blk_e, n_used, w_up, b_up, w_down, b_down, tm)
    return combine_ln(pos, gates.reshape(-1), out_rows, x1, gate_f, ln_g, ln_b, seq_len)


def _layer(x, c, w_ada, b_ada, w_in, kv_norm_g, w_uk, w_uv, grp_norm_dsa, grp_norm_sb, w_out,
           ln1_g, ln1_b, w_router, b_router, w_up, b_up, w_down, b_down, ln2_g, ln2_b):
    B, S, D = x.shape
    n_tok = B * S
    n_dsa = w_uk.shape[0]
    w_qa = n_dsa * HEAD_DIM
    w_qidx = IDX_HEADS * IDX_DIM
    w_sb = grp_norm_sb.shape[0]
    n_sb = w_sb // HEAD_DIM

    mod = ada_modulation(c, w_ada, b_ada)
    shift_a, scale_a, gate_a, shift_f, scale_f, gate_f = jnp.split(mod, 6, axis=-1)

    o1 = w_qa
    o2 = o1 + KV_LATENT
    o3 = o2 + w_qidx
    o4 = o3 + IDX_DIM
    o5 = o4 + IDX_HEADS
    w_main = jnp.concatenate([w_in[:, o2:o3], w_in[:, :o1], w_in[:, o5:]], axis=1).astype(BF16)
    n_small = KV_LATENT + IDX_DIM + IDX_HEADS
    n_small_pad = KV_LATENT + IDX_DIM + LANES
    w_small = jnp.concatenate(
        [w_in[:, o1:o2], w_in[:, o3:o5], jnp.zeros((D, n_small_pad - n_small), F32)], axis=1).astype(BF16)

    proj, ckv, kidx, widx = input_projection(x.reshape(n_tok, D), scale_a, shift_a, w_main, w_small,
                                             kv_norm_g, S)
    proj = proj.reshape(B, S, w_main.shape[1])

    sb_w = SB_HEADS_PER_STEP * HEAD_DIM
    o_dsa = dsa_attention(
        qidx=(proj, 0), kidx=kidx.reshape(B, S, IDX_DIM), widx=(widx.reshape(B, S, LANES), 0),
        qa=(proj, w_qidx // w_qa), ckv=ckv.reshape(B, S, KV_LATENT),
        w_uk=w_uk.astype(BF16), w_uv=w_uv.astype(BF16), g=grp_norm_dsa)
    sb0 = w_qidx + w_qa
    o_sb = stick_breaking_attention((proj, sb0 // sb_w), (proj, (sb0 + w_sb) // sb_w),
                                    (proj, (sb0 + 2 * w_sb) // sb_w), n_sb)

    x1, h2_slabs, logits = outproj_ln(
        o_dsa.reshape(n_tok, w_qa), o_sb.reshape(n_tok, w_sb), grp_norm_sb, w_out.astype(BF16),
        x.reshape(n_tok, D), gate_a, ln1_g, ln1_b, scale_f, shift_f, w_router, S)
    out = moe_block(h2_slabs, logits + b_router.astype(F32), x1, gate_f, ln2_g, ln2_b,
                    w_up, b_up, w_down, b_down, S)
    return out.reshape(B, S, D)


def kernel(x, c, w_ada, b_ada, w_in, kv_norm_g, w_uk, w_uv, grp_norm_dsa, grp_norm_sb, w_out, ln1_g, ln1_b, w_router, b_router, w_up, b_up, w_down, b_down, ln2_g, ln2_b):
    return _layer(x, c, w_ada[0], b_ada[0], w_in[0], kv_norm_g[0], w_uk[0], w_uv[0],
                  grp_norm_dsa[0], grp_norm_sb[0], w_out[0], ln1_g[0], ln1_b[0], w_router[0],
                  b_router[0], w_up[0], b_up[0], w_down[0], b_down[0], ln2_g[0], ln2_b[0])
```

```python
import functools
import math

import jax
import jax.numpy as jnp
from jax import lax
from jax.experimental import pallas as pl
from jax.experimental.pallas import tpu as pltpu

F32 = jnp.float32
BF16 = jnp.bfloat16

HEAD_DIM = 128
KV_LATENT = 512
IDX_HEADS = 32
IDX_DIM = 128
TOPK_MAX = 256
N_EXPERTS = 32
TOP_K_EXPERTS = 4
SWIGLU_ALPHA = 1.702
SWIGLU_LIMIT = 7.0
DEPTH = 1
DEEPNORM_ALPHA = (2 * DEPTH) ** 0.25
EPS = 1e-5
INDEX_SCALE = (IDX_HEADS * IDX_DIM) ** -0.5
ATTN_SCALE = HEAD_DIM ** -0.5
LOG2_E = math.log2(math.e)

LANES = 128
VMEM_LIMIT_BYTES = 56 * 1024 * 1024
VMEM_LIMIT_WIDE_ROWS_BYTES = 60 * 1024 * 1024

MASKED_LOGIT = -1e30
F32_EXP_UNDERFLOW = 104.0

NT_DIMS = (((1,), (1,)), ((), ()))


def _cparams(n_axes):
    return pltpu.CompilerParams(
        dimension_semantics=("arbitrary",) * n_axes, vmem_limit_bytes=VMEM_LIMIT_BYTES)


def _split_bf16(v):
    hi = v.astype(BF16)
    lo = (v - hi.astype(F32)).astype(BF16)
    return hi, lo


def _lane_tile(v, width):
    return jnp.concatenate([v] * (width // LANES), axis=1)


def _ada_kernel(c_ref, w_ref, b_ref, o_ref):
    c = c_ref[...]
    s = c * (1.0 / (1.0 + jnp.exp(-c)))
    s_hi, s_lo = _split_bf16(s)
    w_hi, w_lo = _split_bf16(w_ref[...])
    acc = jnp.dot(s_hi, w_hi, preferred_element_type=F32)
    acc += jnp.dot(s_hi, w_lo, preferred_element_type=F32)
    acc += jnp.dot(s_lo, w_hi, preferred_element_type=F32)
    o_ref[...] = acc + b_ref[...]


def ada_modulation(c, w_ada, b_ada, tn=512):
    B, D = c.shape
    N = w_ada.shape[1]
    rows = 8
    c_pad = jnp.zeros((rows, D), F32).at[:B].set(c)
    out = pl.pallas_call(
        _ada_kernel,
        grid=(N // tn,),
        in_specs=[pl.BlockSpec((rows, D), lambda j: (0, 0)),
                  pl.BlockSpec((D, tn), lambda j: (0, j)),
                  pl.BlockSpec((1, tn), lambda j: (0, j))],
        out_specs=pl.BlockSpec((rows, tn), lambda j: (0, j)),
        out_shape=jax.ShapeDtypeStruct((rows, N), F32),
        compiler_params=_cparams(1),
        name="ada_modulation",
    )(c_pad, w_ada, b_ada.reshape(1, N))
    return out[:B]


def _inproj_kernel(x_ref, sc_ref, sh_ref, wm_ref, ws_ref, g_ref, proj_ref, ckv_ref, kidx_ref, widx_ref,
                   h_ref, *, n_main):
    j = pl.program_id(1)

    @pl.when(j == 0)
    def _():
        h_ref[...] = (x_ref[...] * (1.0 + sc_ref[0]) + sh_ref[0]).astype(BF16)

    @pl.when(j < n_main)
    def _():
        proj_ref[...] = jnp.dot(h_ref[...], wm_ref[...], preferred_element_type=F32).astype(BF16)

    @pl.when(j == n_main)
    def _():
        s = jnp.dot(h_ref[...], ws_ref[...], preferred_element_type=F32)
        lat = s[:, :KV_LATENT]
        ms = jnp.mean(lat * lat, axis=-1, keepdims=True)
        ckv_ref[...] = (lat * lax.rsqrt(ms + EPS) * g_ref[...]).astype(BF16)
        kidx_ref[...] = s[:, KV_LATENT:KV_LATENT + IDX_DIM].astype(BF16)
        widx_ref[...] = s[:, KV_LATENT + IDX_DIM:]


def input_projection(x, scale, shift, w_main, w_small, kv_norm_g, seq_len, tm=512, tn=512):
    M, D = x.shape
    B = scale.shape[0]
    n_mainw = w_main.shape[1]
    tm, tn = min(tm, seq_len), min(tn, n_mainw)
    n_main = n_mainw // tn
    n_small = w_small.shape[1]

    def per_batch(i, j):
        return (i * tm // seq_len, 0, 0)

    row = lambda i, j: (i, 0)
    main_col = lambda i, j: (0, jnp.minimum(j, n_main - 1))
    return pl.pallas_call(
        functools.partial(_inproj_kernel, n_main=n_main),
        grid=(M // tm, n_main + 1),
        in_specs=[
            pl.BlockSpec((tm, D), row),
            pl.BlockSpec((1, 1, D), per_batch),
            pl.BlockSpec((1, 1, D), per_batch),
            pl.BlockSpec((D, tn), main_col),
            pl.BlockSpec((D, n_small), lambda i, j: (0, 0), pipeline_mode=pl.Buffered(1)),
            pl.BlockSpec((1, KV_LATENT), lambda i, j: (0, 0)),
        ],
        out_specs=[
            pl.BlockSpec((tm, tn), lambda i, j: (i, jnp.minimum(j, n_main - 1))),
            pl.BlockSpec((tm, KV_LATENT), row),
            pl.BlockSpec((tm, IDX_DIM), row),
            pl.BlockSpec((tm, LANES), row),
        ],
        out_shape=[jax.ShapeDtypeStruct((M, n_mainw), BF16), jax.ShapeDtypeStruct((M, KV_LATENT), BF16),
                   jax.ShapeDtypeStruct((M, IDX_DIM), BF16), jax.ShapeDtypeStruct((M, LANES), F32)],
        scratch_shapes=[pltpu.VMEM((tm, D), BF16)],
        compiler_params=_cparams(2),
        name="input_projection",
    )(x, scale.reshape(B, 1, D), shift.reshape(B, 1, D), w_main, w_small, kv_norm_g.reshape(1, KV_LATENT))


def _sortable_key(v):
    bits = pltpu.bitcast(v + 0.0, jnp.int32)
    return bits ^ (lax.shift_right_arithmetic(bits, 31) & jnp.int32(0x7FFFFFFF))


def _dsa_kernel(qidx_ref, kidx_ref, widx_ref, qa_ref, ckv_ref, wuk_ref, wuv_ref, g_ref, o_ref,
                key_ref, wb_ref, q2_ref, qlat_ref, lg_ref, bias_ref, p_ref, acc_ref, m_ref, l_ref,
                alpha_ref, slope_ref, tie_end_ref, obuf_ref, *, tq, tk_score, tk, k_sel, n_heads, n_idx):
    s_len = key_ref.shape[1]
    pos_bits = max(1, (s_len - 1).bit_length())
    i = pl.program_id(1)
    t0 = i * tq
    n_att = (t0 + tq - 1) // tk + 1
    n_score = n_att * (tk // tk_score)

    w = widx_ref[0]
    for h in range(n_idx):
        wb_ref[h] = jnp.broadcast_to(w[:, h:h + 1], (tq, LANES))
        q2_ref[h * tq:(h + 1) * tq, :] = qidx_ref[0, :, h * IDX_DIM:(h + 1) * IDX_DIM]
    t_ids_s = t0 + lax.broadcasted_iota(jnp.int32, (tq, tk_score), 0)
    lane_ids_s = lax.broadcasted_iota(jnp.int32, (tq, tk_score), 1)

    def score_chunk(j, carry):
        s0 = pl.multiple_of(j * tk_score, tk_score)
        kb = kidx_ref[0, pl.ds(s0, tk_score), :]
        lg = lax.dot_general(q2_ref[...], kb, NT_DIMS, preferred_element_type=F32)
        r = jnp.maximum(lg, 0.0).reshape(n_idx, tq, tk_score)
        parts = [jnp.sum(r[:, :, c * LANES:(c + 1) * LANES] * wb_ref[...], axis=0)
                 for c in range(tk_score // LANES)]
        sc = jnp.concatenate(parts, axis=1) * INDEX_SCALE
        sc = jnp.where(s0 + lane_ids_s <= t_ids_s, sc, -jnp.inf)
        key_ref[:, pl.ds(s0, tk_score)] = _sortable_key(sc)
        return carry

    lax.fori_loop(0, n_score, score_chunk, 0)

    lane_ids = lax.broadcasted_iota(jnp.int32, (tq, tk), 1)

    def count_where(preds):
        def chunk(j, cnts):
            s0 = pl.multiple_of(j * tk, tk)
            keys = key_ref[:, pl.ds(s0, tk)]
            s_ids = s0 + lane_ids
            hits = [jnp.where(pred(keys, s_ids), 1.0, 0.0) for pred in preds]
            return tuple(cnt + sum(hit[:, c * LANES:(c + 1) * LANES] for c in range(tk // LANES))
                         for cnt, hit in zip(cnts, hits))
        zero = jnp.zeros((tq, LANES), F32)
        cnts = lax.fori_loop(0, n_att, chunk, (zero,) * len(preds))
        return [jnp.sum(cnt, axis=1, keepdims=True) for cnt in cnts]

    def bisect(it, prefix):
        cand = prefix + lax.shift_left(jnp.int32(1), 31 - it)
        cand_t = _lane_tile(cand, tk)
        (total,) = count_where([lambda keys, s_ids: keys >= cand_t])
        return jnp.where(total >= float(k_sel), cand, prefix)

    thr = lax.fori_loop(0, 32, bisect, jnp.full((tq, LANES), -2 ** 31, jnp.int32))
    thr_t = _lane_tile(thr, tk)

    n_above, n_at_least = count_where([lambda keys, s_ids: keys > thr_t,
                                       lambda keys, s_ids: keys >= thr_t])
    need = float(k_sel) - n_above
    tie_end_ref[...] = jnp.full((tq, LANES), s_len, jnp.int32)

    @pl.when(jnp.max(n_at_least) > float(k_sel))
    def _():
        def grow(it, p):
            cand = p | lax.shift_left(jnp.int32(1), pos_bits - 1 - it)
            cand_t = _lane_tile(cand, tk)
            (below,) = count_where([lambda keys, s_ids: (keys == thr_t) & (s_ids < cand_t)])
            return jnp.where(below < need, cand, p)
        tie_end_ref[...] = lax.fori_loop(0, pos_bits, grow, jnp.zeros((tq, LANES), jnp.int32))

    for h in range(n_heads):
        ql = jnp.dot(qa_ref[0, :, h * HEAD_DIM:(h + 1) * HEAD_DIM], wuk_ref[h], preferred_element_type=F32)
        qlat_ref[h * tq:(h + 1) * tq, :] = (ql * (ATTN_SCALE * LOG2_E)).astype(BF16)
    head_no = lax.broadcasted_iota(jnp.int32, slope_ref.shape, 0).astype(F32)
    slope_ref[...] = jnp.exp2(-8.0 * (head_no + 1.0) / n_heads) * LOG2_E
    m_ref[...] = jnp.full(m_ref.shape, MASKED_LOGIT, F32)
    l_ref[...] = jnp.zeros(l_ref.shape, F32)
    acc_ref[...] = jnp.zeros(acc_ref.shape, F32)
    t_ids = t0 + lax.broadcasted_iota(jnp.int32, (tq, tk), 0)
    col_ids = lax.broadcasted_iota(jnp.int32, (1, tk), 1)

    def attend_chunk(j, carry):
        s0 = pl.multiple_of(j * tk, tk)
        cb = ckv_ref[0, pl.ds(s0, tk), :]
        lg_ref[...] = lax.dot_general(qlat_ref[...], cb, NT_DIMS, preferred_element_type=F32)
        keys = key_ref[:, pl.ds(s0, tk)]
        s_ids = s0 + lane_ids
        sel = (keys > thr_t) | ((keys == thr_t) & (s_ids <= _lane_tile(tie_end_ref[...], tk)))
        sel = sel & (s_ids <= t_ids)
        bias_ref[...] = jnp.where(sel, 0.0, MASKED_LOGIT)
        rel = (s0 - t0 + col_ids).astype(F32)

        def head(h, c2):
            rows = pl.ds(pl.multiple_of(h * tq, tq), tq)
            x = lg_ref[rows, :] + _lane_tile(slope_ref[h], tk) * rel + bias_ref[...]
            m_old = m_ref[rows, :]
            m_new = jnp.maximum(m_old, jnp.max(x, axis=1, keepdims=True))
            alpha = jnp.exp2(m_old - m_new)
            p = jnp.exp2(x - _lane_tile(m_new, tk))
            l_ref[rows, :] = alpha * l_ref[rows, :] + jnp.sum(p, axis=1, keepdims=True)
            m_ref[rows, :] = m_new
            alpha_ref[rows, :] = alpha
            p_ref[rows, :] = p.astype(BF16)
            return c2

        lax.fori_loop(0, n_heads, head, 0, unroll=True)
        lat = acc_ref.shape[1]
        acc_ref[...] = (acc_ref[...] * _lane_tile(alpha_ref[...], lat)
                        + jnp.dot(p_ref[...], cb, preferred_element_type=F32))
        return carry

    lax.fori_loop(0, n_att, attend_chunk, 0)

    ss = jnp.zeros((tq, 1), F32)
    for h in range(n_heads):
        rows = slice(h * tq, (h + 1) * tq)
        inv_l = _lane_tile(1.0 / l_ref[rows, :], acc_ref.shape[1])
        oh = jnp.dot((acc_ref[rows, :] * inv_l).astype(BF16), wuv_ref[h],
                     preferred_element_type=F32)
        obuf_ref[:, h * HEAD_DIM:(h + 1) * HEAD_DIM] = oh
        ss = ss + jnp.sum(oh * oh, axis=1, keepdims=True)
    inv = lax.rsqrt(ss * (1.0 / (n_heads * HEAD_DIM)) + EPS)
    o_ref[0] = (obuf_ref[...] * inv * g_ref[...]).astype(o_ref.dtype)


def dsa_attention(qidx, kidx, widx, qa, ckv, w_uk, w_uv, g, tq=128, tk_score=512, tk=512):
    qidx_arr, qidx_blk = qidx
    qa_arr, qa_blk = qa
    widx_arr, widx_blk = widx
    B, S, lat = ckv.shape
    n_heads = w_uk.shape[0]
    n_idx = IDX_HEADS
    tq, tk = min(tq, S), min(tk, S)
    tk_score = min(tk_score, tk)
    k_sel = min(TOPK_MAX, S // 4)
    width = n_heads * HEAD_DIM
    kernel = functools.partial(_dsa_kernel, tq=tq, tk_score=tk_score, tk=tk, k_sel=k_sel,
                               n_heads=n_heads, n_idx=n_idx)
    once = pl.Buffered(1)
    return pl.pallas_call(
        kernel,
        grid=(B, S // tq),
        in_specs=[
            pl.BlockSpec((1, tq, n_idx * IDX_DIM), lambda b, i: (b, i, qidx_blk)),
            pl.BlockSpec((1, S, IDX_DIM), lambda b, i: (b, 0, 0), pipeline_mode=once),
            pl.BlockSpec((1, tq, LANES), lambda b, i: (b, i, widx_blk)),
            pl.BlockSpec((1, tq, width), lambda b, i: (b, i, qa_blk)),
            pl.BlockSpec((1, S, lat), lambda b, i: (b, 0, 0), pipeline_mode=once),
            pl.BlockSpec((n_heads, HEAD_DIM, lat), lambda b, i: (0, 0, 0), pipeline_mode=once),
            pl.BlockSpec((n_heads, lat, HEAD_DIM), lambda b, i: (0, 0, 0), pipeline_mode=once),
            pl.BlockSpec((1, width), lambda b, i: (0, 0)),
        ],
        out_specs=pl.BlockSpec((1, tq, width), lambda b, i: (b, i, 0)),
        out_shape=jax.ShapeDtypeStruct((B, S, width), BF16),
        scratch_shapes=[
            pltpu.VMEM((tq, S), jnp.int32),
            pltpu.VMEM((n_idx, tq, LANES), F32),
            pltpu.VMEM((n_idx * tq, IDX_DIM), BF16),
            pltpu.VMEM((n_heads * tq, lat), BF16),
            pltpu.VMEM((n_heads * tq, tk), F32),
            pltpu.VMEM((tq, tk), F32),
            pltpu.VMEM((n_heads * tq, tk), BF16),
            pltpu.VMEM((n_heads * tq, lat), F32),
            pltpu.VMEM((n_heads * tq, LANES), F32),
            pltpu.VMEM((n_heads * tq, LANES), F32),
            pltpu.VMEM((n_heads * tq, LANES), F32),
            pltpu.VMEM((n_heads, 1, LANES), F32),
            pltpu.VMEM((tq, LANES), jnp.int32),
            pltpu.VMEM((tq, width), F32),
        ],
        compiler_params=_cparams(2),
        name="dsa_attention",
    )(qidx_arr, kidx, widx_arr, qa_arr, ckv, w_uk, w_uv, g.reshape(1, width))


def _sb_kernel(q_ref, k_ref, v_ref, o_ref, acc_ref, carry_ref, *, tq, hb):
    i = pl.program_id(2)
    t0 = i * tq
    row = lax.broadcasted_iota(jnp.int32, (tq, tq), 0)
    col = lax.broadcasted_iota(jnp.int32, (tq, tq), 1)
    suffix_ones = jnp.where(row > col, 1.0, 0.0).astype(BF16)
    acc_ref[...] = jnp.zeros(acc_ref.shape, F32)
    carry_ref[...] = jnp.zeros(carry_ref.shape, F32)

    def keep_going(state):
        j, carry_min = state
        return (j >= 0) & (carry_min <= F32_EXP_UNDERFLOW)

    def block(state):
        j, _ = state
        s0 = pl.multiple_of(j * tq, tq)
        strict = (s0 + col) < (t0 + row)
        carry_min = None
        for h in range(hb):
            cols = slice(h * HEAD_DIM, (h + 1) * HEAD_DIM)
            kb = k_ref[0, pl.ds(s0, tq), cols]
            vb = v_ref[0, pl.ds(s0, tq), cols]
            z = lax.dot_general(q_ref[0, :, cols], kb, NT_DIMS, preferred_element_type=F32) * ATTN_SCALE
            sp = jnp.maximum(z, 0.0) + jnp.log(1.0 + jnp.exp(-jnp.abs(z)))
            u = jnp.where(strict, sp, 0.0)
            u_hi, u_lo = _split_bf16(u)
            within = (jnp.dot(u_hi, suffix_ones, preferred_element_type=F32)
                      + jnp.dot(u_lo, suffix_ones, preferred_element_type=F32))
            carry = carry_ref[:, cols]
            a = jnp.where(strict, jnp.exp(z - sp - (_lane_tile(carry, tq) + within)), 0.0)
            acc_ref[:, cols] += jnp.dot(a.astype(BF16), vb, preferred_element_type=F32)
            carry = carry + jnp.sum(u, axis=1, keepdims=True)
            carry_ref[:, cols] = carry
            head_min = jnp.min(carry)
            carry_min = head_min if carry_min is None else jnp.minimum(carry_min, head_min)
        return j - 1, carry_min

    lax.while_loop(keep_going, block, (i, jnp.float32(0.0)))
    o_ref[0] = acc_ref[...]


SB_HEADS_PER_STEP = 8


def stick_breaking_attention(q, k, v, n_heads, tq=256, hb=SB_HEADS_PER_STEP):
    (q_arr, q_blk), (k_arr, k_blk), (v_arr, v_blk) = q, k, v
    B, S, _ = q_arr.shape
    tq = min(tq, S)
    wb = hb * HEAD_DIM
    once = pl.Buffered(1)
    return pl.pallas_call(
        functools.partial(_sb_kernel, tq=tq, hb=hb),
        grid=(B, n_heads // hb, S // tq),
        in_specs=[pl.BlockSpec((1, tq, wb), lambda b, h, i: (b, i, q_blk + h)),
                  pl.BlockSpec((1, S, wb), lambda b, h, i: (b, 0, k_blk + h), pipeline_mode=once),
                  pl.BlockSpec((1, S, wb), lambda b, h, i: (b, 0, v_blk + h), pipeline_mode=once)],
        out_specs=pl.BlockSpec((1, tq, wb), lambda b, h, i: (b, i, h)),
        out_shape=jax.ShapeDtypeStruct((B, S, n_heads * HEAD_DIM), F32),
        scratch_shapes=[pltpu.VMEM((tq, wb), F32), pltpu.VMEM((tq, wb), F32)],
        compiler_params=_cparams(3),
        name="stick_breaking_attention",
    )(q_arr, k_arr, v_arr)


def _layer_norm(v, g, b):
    mu = jnp.mean(v, axis=-1, keepdims=True)
    d = v - mu
    var = jnp.mean(d * d, axis=-1, keepdims=True)
    return d * lax.rsqrt(var + EPS) * g + b


def _outproj_ln_kernel(od_ref, os_ref, gsb_ref, w_ref, x_ref, gate_ref, g_ref, b_ref, sc_ref, sh_ref,
                       wr_ref, x1_ref, h2_ref, lg_ref, a_ref, *, tn):
    j = pl.program_id(1)
    wd = od_ref.shape[1]

    @pl.when(j == 0)
    def _():
        a_ref[:, :wd] = od_ref[...]
        o = os_ref[...]
        ms = jnp.mean(o * o, axis=-1, keepdims=True)
        a_ref[:, wd:] = (o * lax.rsqrt(ms + EPS) * gsb_ref[...]).astype(BF16)

    x1_ref[:, pl.ds(pl.multiple_of(j * tn, tn), tn)] = jnp.dot(
        a_ref[...], w_ref[...], preferred_element_type=F32)

    @pl.when(j == pl.num_programs(1) - 1)
    def _():
        y = _layer_norm(DEEPNORM_ALPHA * x_ref[...] + gate_ref[0] * x1_ref[...], g_ref[...], b_ref[...])
        x1_ref[...] = y
        h2 = (y * (1.0 + sc_ref[0]) + sh_ref[0]).astype(BF16)
        lg_ref[...] = jnp.dot(h2, wr_ref[...], preferred_element_type=F32)
        h2_ref[...] = pltpu.einshape("t(sl)->tsl", h2, s=h2_ref.shape[1])


def outproj_ln(o_dsa, o_sb, g_sb, w_out, x, gate, ln_g, ln_b, scale, shift, w_router, seq_len,
               tm=512, tn=256):
    M, D = x.shape
    wd, ws = o_dsa.shape[1], o_sb.shape[1]
    B = gate.shape[0]
    tm, tn = min(tm, seq_len), min(tn, D)
    n_exp = w_router.shape[1]
    wr = jnp.zeros((D, LANES), BF16).at[:, :n_exp].set(w_router.astype(BF16))

    def per_batch(i, j):
        return (i * tm // seq_len, 0, 0)

    row = lambda i, j: (i, 0)
    shared = lambda i, j: (0, 0)
    once = pl.Buffered(1)
    x1, h2_slabs, logits = pl.pallas_call(
        functools.partial(_outproj_ln_kernel, tn=tn),
        grid=(M // tm, D // tn),
        in_specs=[
            pl.BlockSpec((tm, wd), row),
            pl.BlockSpec((tm, ws), row, pipeline_mode=once),
            pl.BlockSpec((1, ws), shared),
            pl.BlockSpec((wd + ws, tn), lambda i, j: (0, j)),
            pl.BlockSpec((tm, D), row, pipeline_mode=once),
            pl.BlockSpec((1, 1, D), per_batch),
            pl.BlockSpec((1, D), shared),
            pl.BlockSpec((1, D), shared),
            pl.BlockSpec((1, 1, D), per_batch),
            pl.BlockSpec((1, 1, D), per_batch),
            pl.BlockSpec((D, LANES), shared),
        ],
        out_specs=[pl.BlockSpec((tm, D), row),
                   pl.BlockSpec((tm, D // LANES, LANES), lambda i, j: (i, 0, 0)),
                   pl.BlockSpec((tm, LANES), row)],
        out_shape=[jax.ShapeDtypeStruct((M, D), F32),
                   jax.ShapeDtypeStruct((M, D // LANES, LANES), BF16),
                   jax.ShapeDtypeStruct((M, LANES), F32)],
        scratch_shapes=[pltpu.VMEM((tm, wd + ws), BF16)],
        compiler_params=pltpu.CompilerParams(
            dimension_semantics=("arbitrary", "arbitrary"), vmem_limit_bytes=VMEM_LIMIT_WIDE_ROWS_BYTES),
        name="outproj_ln",
    )(o_dsa, o_sb, g_sb.reshape(1, ws), w_out, x, gate.reshape(B, 1, D), ln_g.reshape(1, D),
      ln_b.reshape(1, D), scale.reshape(B, 1, D), shift.reshape(B, 1, D), wr)
    return x1, h2_slabs, logits[:, :n_exp]


def _scatter_rows_kernel(pos_ref, pad_end_ref, cnt_ref, h_ref, xg_ref, zero_ref, sem, *, tm, tt, top_k,
                         n_exp):
    i = pl.program_id(0)

    def fill_copy(e):
        start = pl.multiple_of(pad_end_ref[e] - tm, tm)
        return pltpu.make_async_copy(zero_ref, xg_ref.at[pl.ds(start, tm)], sem.at[0])

    @pl.when(i == 0)
    def _():
        zero_ref[...] = jnp.zeros(zero_ref.shape, zero_ref.dtype)

        def start_fill(e, c):
            @pl.when(cnt_ref[e] % tm != 0)
            def _():
                fill_copy(e).start()
            return c

        def wait_fill(e, c):
            @pl.when(cnt_ref[e] % tm != 0)
            def _():
                fill_copy(e).wait()
            return c

        def tail_copy(r):
            return pltpu.make_async_copy(
                zero_ref, xg_ref.at[pl.ds(pl.multiple_of(r * tm, tm), tm)], sem.at[0])

        def start_tail(r, c):
            tail_copy(r).start()
            return c

        def wait_tail(r, c):
            tail_copy(r).wait()
            return c

        n_used = pad_end_ref[n_exp - 1] // tm
        n_blocks = xg_ref.shape[0] // tm
        lax.fori_loop(0, n_exp, start_fill, 0)
        lax.fori_loop(n_used, n_blocks, start_tail, 0)
        lax.fori_loop(0, n_exp, wait_fill, 0)
        lax.fori_loop(n_used, n_blocks, wait_tail, 0)

    base = i * tt

    def row_copy(t, k):
        return pltpu.make_async_copy(h_ref.at[t], xg_ref.at[pos_ref[(base + t) * top_k + k]], sem.at[1])

    def start_rows(t, c):
        for k in range(top_k):
            row_copy(t, k).start(priority=k % 2)
        return c

    def wait_rows(t, c):
        for k in range(top_k):
            row_copy(t, k).wait()
        return c

    lax.fori_loop(0, tt, start_rows, 0, unroll=4)
    lax.fori_loop(0, tt, wait_rows, 0, unroll=4)


def scatter_rows(pos, pad_end, counts, h_slabs, n_rows, tm, tt=256):
    n_tok, P, _ = h_slabs.shape
    dtype = h_slabs.dtype
    tt = min(tt, n_tok)
    kernel = functools.partial(_scatter_rows_kernel, tm=tm, tt=tt, top_k=TOP_K_EXPERTS, n_exp=N_EXPERTS)
    return pl.pallas_call(
        kernel,
        grid_spec=pltpu.PrefetchScalarGridSpec(
            num_scalar_prefetch=3,
            grid=(n_tok // tt,),
            in_specs=[pl.BlockSpec((tt, P, LANES), lambda i, p, e, c: (i, 0, 0))],
            out_specs=pl.BlockSpec(memory_space=pl.ANY),
            scratch_shapes=[pltpu.VMEM((tm, P, LANES), dtype), pltpu.SemaphoreType.DMA((2,))],
        ),
        out_shape=jax.ShapeDtypeStruct((n_rows, P, LANES), dtype),
        compiler_params=_cparams(1),
        name="scatter_rows",
    )(pos, pad_end, counts, h_slabs)


def _expert_changed(blk_e_ref, r):
    return (r == 0) | (blk_e_ref[r] != blk_e_ref[jnp.maximum(r - 1, 0)])


def _moe_up_kernel(blk_e_ref, n_used_ref, x_ref, wg_ref, wl_ref, bg_ref, bl_ref, o_ref, wg_bf, wl_bf):
    r = pl.program_id(1)
    used = r < n_used_ref[0]

    @pl.when(used & _expert_changed(blk_e_ref, r))
    def _():
        wg_bf[...] = wg_ref[0].astype(BF16)
        wl_bf[...] = wl_ref[0].astype(BF16)

    @pl.when(used)
    def _():
        x = pltpu.einshape("tsl->t(sl)", x_ref[...])
        glu = jnp.dot(x, wg_bf[...], preferred_element_type=F32) + bg_ref[0]
        lin = jnp.dot(x, wl_bf[...], preferred_element_type=F32) + bl_ref[0]
        glu = jnp.minimum(glu, SWIGLU_LIMIT)
        lin = jnp.clip(lin, -SWIGLU_LIMIT, SWIGLU_LIMIT)
        act = glu * (1.0 / (1.0 + jnp.exp(-SWIGLU_ALPHA * glu))) * (lin + 1.0)
        o_ref[...] = act.astype(o_ref.dtype)

    @pl.when(jnp.logical_not(used))
    def _():
        o_ref[...] = jnp.zeros(o_ref.shape, o_ref.dtype)


def _moe_down_kernel(blk_e_ref, n_used_ref, a_ref, wd_ref, bd_ref, o_ref, wd_bf):
    r = pl.program_id(1)
    used = r < n_used_ref[0]

    @pl.when(used & _expert_changed(blk_e_ref, r))
    def _():
        wd_bf[...] = wd_ref[0].astype(BF16)

    @pl.when(used)
    def _():
        out = jnp.dot(a_ref[...], wd_bf[...], preferred_element_type=F32) + bd_ref[0]
        o_ref[...] = pltpu.einshape("t(sl)->tsl", out, s=o_ref.shape[1])

    @pl.when(jnp.logical_not(used))
    def _():
        o_ref[...] = jnp.zeros(o_ref.shape, o_ref.dtype)


def moe_experts(xg, blk_e, n_used, w_up, b_up, w_down, b_down, tm, tn_up=512, tn_down=2048):
    R, P, _ = xg.shape
    D = P * LANES
    n_exp, d_exp = w_down.shape[0], w_down.shape[1]
    tn_up, tn_down = min(tn_up, d_exp), min(tn_down, D)
    n_up = d_exp // tn_up

    def row(r, n):
        return jnp.minimum(r, n[0] - 1)

    def expert(r, e, n):
        return e[row(r, n)]

    b_up3 = b_up.reshape(n_exp, 1, 2 * d_exp)
    act = pl.pallas_call(
        _moe_up_kernel,
        grid_spec=pltpu.PrefetchScalarGridSpec(
            num_scalar_prefetch=2,
            grid=(n_up, R // tm),
            in_specs=[
                pl.BlockSpec((tm, P, LANES), lambda c, r, e, n: (row(r, n), 0, 0)),
                pl.BlockSpec((1, D, tn_up), lambda c, r, e, n: (expert(r, e, n), 0, c)),
                pl.BlockSpec((1, D, tn_up), lambda c, r, e, n: (expert(r, e, n), 0, n_up + c)),
                pl.BlockSpec((1, 1, tn_up), lambda c, r, e, n: (expert(r, e, n), 0, c)),
                pl.BlockSpec((1, 1, tn_up), lambda c, r, e, n: (expert(r, e, n), 0, n_up + c)),
            ],
            out_specs=pl.BlockSpec((tm, tn_up), lambda c, r, e, n: (r, c)),
            scratch_shapes=[pltpu.VMEM((D, tn_up), BF16), pltpu.VMEM((D, tn_up), BF16)],
        ),
        out_shape=jax.ShapeDtypeStruct((R, d_exp), BF16),
        compiler_params=_cparams(2),
        name="moe_up",
    )(blk_e, n_used, xg, w_up, w_up, b_up3, b_up3)
    return pl.pallas_call(
        _moe_down_kernel,
        grid_spec=pltpu.PrefetchScalarGridSpec(
            num_scalar_prefetch=2,
            grid=(D // tn_down, R // tm),
            in_specs=[
                pl.BlockSpec((tm, d_exp), lambda c, r, e, n: (row(r, n), 0)),
                pl.BlockSpec((1, d_exp, tn_down), lambda c, r, e, n: (expert(r, e, n), 0, c)),
                pl.BlockSpec((1, 1, tn_down), lambda c, r, e, n: (expert(r, e, n), 0, c)),
            ],
            out_specs=pl.BlockSpec((tm, tn_down // LANES, LANES), lambda c, r, e, n: (r, c, 0)),
            scratch_shapes=[pltpu.VMEM((d_exp, tn_down), BF16)],
        ),
        out_shape=jax.ShapeDtypeStruct((R, D // LANES, LANES), F32),
        compiler_params=_cparams(2),
        name="moe_down",
    )(blk_e, n_used, act, w_down, b_down.reshape(n_exp, 1, D))


def _combine_ln_kernel(pos_ref, gates_ref, rows_ref, x_ref, gate_ref, g_ref, b_ref, o_ref,
                       buf0, buf1, ysl_ref, y_ref, sem, *, tt, top_k):
    i = pl.program_id(0)
    n_blocks = pl.num_programs(0)

    def row_copy(blk, buf, slot, t, k):
        src = rows_ref.at[pos_ref[(blk * tt + t) * top_k + k]]
        return pltpu.make_async_copy(src, buf.at[k * tt + t], sem.at[slot])

    def start_block(blk, buf, slot):
        def body(t, c):
            for k in range(top_k):
                row_copy(blk, buf, slot, t, k).start(priority=k % 2)
            return c
        lax.fori_loop(0, tt, body, 0, unroll=4)

    def wait_block(blk, buf, slot):
        def body(t, c):
            for k in range(top_k):
                row_copy(blk, buf, slot, t, k).wait()
            return c
        lax.fori_loop(0, tt, body, 0, unroll=4)

    def combine(buf):
        def per_token(t, c):
            base = (i * tt + t) * top_k
            acc = gates_ref[base] * buf[t]
            for k in range(1, top_k):
                acc = acc + gates_ref[base + k] * buf[k * tt + t]
            ysl_ref[t] = acc
            return c
        lax.fori_loop(0, tt, per_token, 0, unroll=4)
        y_ref[...] = pltpu.einshape("tsl->t(sl)", ysl_ref[...])

    even = i % 2 == 0

    @pl.when(i == 0)
    def _():
        start_block(0, buf0, 0)

    @pl.when((i + 1 < n_blocks) & even)
    def _():
        start_block(i + 1, buf1, 1)

    @pl.when((i + 1 < n_blocks) & jnp.logical_not(even))
    def _():
        start_block(i + 1, buf0, 0)

    @pl.when(even)
    def _():
        wait_block(i, buf0, 0)
        combine(buf0)

    @pl.when(jnp.logical_not(even))
    def _():
        wait_block(i, buf1, 1)
        combine(buf1)

    o_ref[...] = _layer_norm(DEEPNORM_ALPHA * x_ref[...] + gate_ref[0] * y_ref[...], g_ref[...], b_ref[...])


def combine_ln(pos, gates, out_rows, x, gate, ln_g, ln_b, seq_len, tt=128):
    n_tok, D = x.shape
    B = gate.shape[0]
    tt = min(tt, seq_len)
    P = out_rows.shape[1]
    kernel = functools.partial(_combine_ln_kernel, tt=tt, top_k=TOP_K_EXPERTS)
    return pl.pallas_call(
        kernel,
        grid_spec=pltpu.PrefetchScalarGridSpec(
            num_scalar_prefetch=2,
            grid=(n_tok // tt,),
            in_specs=[
                pl.BlockSpec(memory_space=pl.ANY),
                pl.BlockSpec((tt, D), lambda i, p, q: (i, 0)),
                pl.BlockSpec((1, 1, D), lambda i, p, q: (i * tt // seq_len, 0, 0)),
                pl.BlockSpec((1, D), lambda i, p, q: (0, 0)),
                pl.BlockSpec((1, D), lambda i, p, q: (0, 0)),
            ],
            out_specs=pl.BlockSpec((tt, D), lambda i, p, q: (i, 0)),
            scratch_shapes=[pltpu.VMEM((TOP_K_EXPERTS * tt, P, LANES), F32),
                            pltpu.VMEM((TOP_K_EXPERTS * tt, P, LANES), F32),
                            pltpu.VMEM((tt, P, LANES), F32),
                            pltpu.VMEM((tt, D), F32),
                            pltpu.SemaphoreType.DMA((2,))],
        ),
        out_shape=jax.ShapeDtypeStruct((n_tok, D), F32),
        compiler_params=_cparams(1),
        name="combine_ln",
    )(pos, gates, out_rows, x, gate.reshape(B, 1, D), ln_g.reshape(1, D), ln_b.reshape(1, D))


def route(logits, tm):
    n_tok = logits.shape[0]
    top_logit, top_e = lax.top_k(logits, TOP_K_EXPERTS)
    gates = jax.nn.softmax(top_logit, axis=-1)
    n_asg = n_tok * TOP_K_EXPERTS
    e_flat = top_e.reshape(n_asg)
    onehot = (e_flat[:, None] == jnp.arange(N_EXPERTS, dtype=e_flat.dtype)[None, :]).astype(jnp.int32)
    assert n_asg % LANES == 0
    blocks = onehot.reshape(n_asg // LANES, LANES, N_EXPERTS).astype(F32)
    lower = jnp.tril(jnp.ones((LANES, LANES), F32))
    within = jnp.einsum('ij,bjk->bik', lower, blocks)
    block_tot = within[:, -1, :]
    before = jnp.cumsum(block_tot, axis=0) - block_tot
    running = (within + before[:, None, :]).reshape(n_asg, N_EXPERTS).astype(jnp.int32)
    rank = jnp.take_along_axis(running, e_flat[:, None], axis=1)[:, 0] - 1
    counts = running[-1]
    padded = (counts + tm - 1) // tm * tm
    pad_end = jnp.cumsum(padded)
    pad_start = pad_end - padded
    pos = (pad_start[e_flat] + rank).astype(jnp.int32)
    n_blocks = -(-n_asg // tm) + N_EXPERTS
    first_row = jnp.arange(n_blocks, dtype=pad_end.dtype) * tm
    blk_e = jnp.minimum(jnp.sum(pad_end[None, :] <= first_row[:, None], axis=1),
                        N_EXPERTS - 1).astype(jnp.int32)
    n_used = (pad_end[-1] // tm).astype(jnp.int32).reshape(1)
    return (gates, pos, blk_e, n_used, pad_end.astype(jnp.int32), counts.astype(jnp.int32),
            n_blocks * tm)


def moe_block(h2_slabs, logits, x1, gate_f, ln_g, ln_b, w_up, b_up, w_down, b_down, seq_len, tm=512):
    gates, pos, blk_e, n_used, pad_end, counts, n_rows = route(logits, tm)
    xg = scatter_rows(pos, pad_end, counts, h2_slabs, n_rows, tm)
    out_rows = moe_experts(xg, blk_e, n_used, w_up, b_up, w_down, b_down, tm)
    return combine_ln(pos, gates.reshape(-1), out_rows, x1, gate_f, ln_g, ln_b, seq_len)


def _layer(x, c, w_ada, b_ada, w_in, kv_norm_g, w_uk, w_uv, grp_norm_dsa, grp_norm_sb, w_out,
           ln1_g, ln1_b, w_router, b_router, w_up, b_up, w_down, b_down, ln2_g, ln2_b):
    B, S, D = x.shape
    n_tok = B * S
    n_dsa = w_uk.shape[0]
    w_qa = n_dsa * HEAD_DIM
    w_qidx = IDX_HEADS * IDX_DIM
    w_sb = grp_norm_sb.shape[0]
    n_sb = w_sb // HEAD_DIM

    mod = ada_modulation(c, w_ada, b_ada)
    shift_a, scale_a, gate_a, shift_f, scale_f, gate_f = jnp.split(mod, 6, axis=-1)

    o1 = w_qa
    o2 = o1 + KV_LATENT
    o3 = o2 + w_qidx
    o4 = o3 + IDX_DIM
    o5 = o4 + IDX_HEADS
    w_main = jnp.concatenate([w_in[:, o2:o3], w_in[:, :o1], w_in[:, o5:]], axis=1).astype(BF16)
    n_small = KV_LATENT + IDX_DIM + IDX_HEADS
    n_small_pad = KV_LATENT + IDX_DIM + LANES
    w_small = jnp.concatenate(
        [w_in[:, o1:o2], w_in[:, o3:o5], jnp.zeros((D, n_small_pad - n_small), F32)], axis=1).astype(BF16)

    proj, ckv, kidx, widx = input_projection(x.reshape(n_tok, D), scale_a, shift_a, w_main, w_small,
                                             kv_norm_g, S)
    proj = proj.reshape(B, S, w_main.shape[1])

    sb_w = SB_HEADS_PER_STEP * HEAD_DIM
    o_dsa = dsa_attention(
        qidx=(proj, 0), kidx=kidx.reshape(B, S, IDX_DIM), widx=(widx.reshape(B, S, LANES), 0),
        qa=(proj, w_qidx // w_qa), ckv=ckv.reshape(B, S, KV_LATENT),
        w_uk=w_uk.astype(BF16), w_uv=w_uv.astype(BF16), g=grp_norm_dsa)
    sb0 = w_qidx + w_qa
    o_sb = stick_breaking_attention((proj, sb0 // sb_w), (proj, (sb0 + w_sb) // sb_w),
                                    (proj, (sb0 + 2 * w_sb) // sb_w), n_sb)

    x1, h2_slabs, logits = outproj_ln(
        o_dsa.reshape(n_tok, w_qa), o_sb.reshape(n_tok, w_sb), grp_norm_sb, w_out.astype(BF16),
        x.reshape(n_tok, D), gate_a, ln1_g, ln1_b, scale_f, shift_f, w_router, S)
    out = moe_block(h2_slabs, logits + b_router.astype(F32), x1, gate_f, ln2_g, ln2_b,
                    w_up, b_up, w_down, b_down, S)
    return out.reshape(B, S, D)


def kernel(x, c, w_ada, b_ada, w_in, kv_norm_g, w_uk, w_uv, grp_norm_dsa, grp_norm_sb, w_out, ln1_g, ln1_b, w_router, b_router, w_up, b_up, w_down, b_down, ln2_g, ln2_b):
    return _layer(x, c, w_ada[0], b_ada[0], w_in[0], kv_norm_g[0], w_uk[0], w_uv[0],
                  grp_norm_dsa[0], grp_norm_sb[0], w_out[0], ln1_g[0], ln1_b[0], w_router[0],
                  b_router[0], w_up[0], b_up[0], w_down[0], b_down[0], ln2_g[0], ln2_b[0])
```

```python
import functools
import math

import jax
import jax.numpy as jnp
from jax import lax
from jax.experimental import pallas as pl
from jax.experimental.pallas import tpu as pltpu

F32 = jnp.float32
BF16 = jnp.bfloat16

HEAD_DIM = 128
KV_LATENT = 512
IDX_HEADS = 32
IDX_DIM = 128
TOPK_MAX = 256
N_EXPERTS = 32
TOP_K_EXPERTS = 4
SWIGLU_ALPHA = 1.702
SWIGLU_LIMIT = 7.0
DEPTH = 1
DEEPNORM_ALPHA = (2 * DEPTH) ** 0.25
EPS = 1e-5
INDEX_SCALE = (IDX_HEADS * IDX_DIM) ** -0.5
ATTN_SCALE = HEAD_DIM ** -0.5
LOG2_E = math.log2(math.e)

LANES = 128
VMEM_LIMIT_BYTES = 56 * 1024 * 1024
VMEM_LIMIT_WIDE_ROWS_BYTES = 60 * 1024 * 1024

MASKED_LOGIT = -1e30
F32_EXP_UNDERFLOW = 104.0

NT_DIMS = (((1,), (1,)), ((), ()))


def _cparams(n_axes):
    return pltpu.CompilerParams(
        dimension_semantics=("arbitrary",) * n_axes, vmem_limit_bytes=VMEM_LIMIT_BYTES)


def _split_bf16(v):
    hi = v.astype(BF16)
    lo = (v - hi.astype(F32)).astype(BF16)
    return hi, lo


def _lane_tile(v, width):
    return jnp.concatenate([v] * (width // LANES), axis=1)


def _ada_kernel(c_ref, w_ref, b_ref, o_ref):
    c = c_ref[...]
    s = c * (1.0 / (1.0 + jnp.exp(-c)))
    s_hi, s_lo = _split_bf16(s)
    w_hi, w_lo = _split_bf16(w_ref[...])
    acc = jnp.dot(s_hi, w_hi, preferred_element_type=F32)
    acc += jnp.dot(s_hi, w_lo, preferred_element_type=F32)
    acc += jnp.dot(s_lo, w_hi, preferred_element_type=F32)
    o_ref[...] = acc + b_ref[...]


def ada_modulation(c, w_ada, b_ada, tn=512):
    B, D = c.shape
    N = w_ada.shape[1]
    rows = 8
    c_pad = jnp.zeros((rows, D), F32).at[:B].set(c)
    out = pl.pallas_call(
        _ada_kernel,
        grid=(N // tn,),
        in_specs=[pl.BlockSpec((rows, D), lambda j: (0, 0)),
                  pl.BlockSpec((D, tn), lambda j: (0, j)),
                  pl.BlockSpec((1, tn), lambda j: (0, j))],
        out_specs=pl.BlockSpec((rows, tn), lambda j: (0, j)),
        out_shape=jax.ShapeDtypeStruct((rows, N), F32),
        compiler_params=_cparams(1),
        name="ada_modulation",
    )(c_pad, w_ada, b_ada.reshape(1, N))
    return out[:B]


def _inproj_kernel(x_ref, sc_ref, sh_ref, wm_ref, ws_ref, g_ref, proj_ref, ckv_ref, kidx_ref, widx_ref,
                   h_ref, *, n_main):
    j = pl.program_id(1)

    @pl.when(j == 0)
    def _():
        h_ref[...] = (x_ref[...] * (1.0 + sc_ref[0]) + sh_ref[0]).astype(BF16)

    @pl.when(j < n_main)
    def _():
        proj_ref[...] = jnp.dot(h_ref[...], wm_ref[...], preferred_element_type=F32).astype(BF16)

    @pl.when(j == n_main)
    def _():
        s = jnp.dot(h_ref[...], ws_ref[...], preferred_element_type=F32)
        lat = s[:, :KV_LATENT]
        ms = jnp.mean(lat * lat, axis=-1, keepdims=True)
        ckv_ref[...] = (lat * lax.rsqrt(ms + EPS) * g_ref[...]).astype(BF16)
        kidx_ref[...] = s[:, KV_LATENT:KV_LATENT + IDX_DIM].astype(BF16)
        widx_ref[...] = s[:, KV_LATENT + IDX_DIM:]


def input_projection(x, scale, shift, w_main, w_small, kv_norm_g, seq_len, tm=512, tn=512):
    M, D = x.shape
    B = scale.shape[0]
    n_mainw = w_main.shape[1]
    tm, tn = min(tm, seq_len), min(tn, n_mainw)
    n_main = n_mainw // tn
    n_small = w_small.shape[1]

    def per_batch(i, j):
        return (i * tm // seq_len, 0, 0)

    row = lambda i, j: (i, 0)
    main_col = lambda i, j: (0, jnp.minimum(j, n_main - 1))
    return pl.pallas_call(
        functools.partial(_inproj_kernel, n_main=n_main),
        grid=(M // tm, n_main + 1),
        in_specs=[
            pl.BlockSpec((tm, D), row),
            pl.BlockSpec((1, 1, D), per_batch),
            pl.BlockSpec((1, 1, D), per_batch),
            pl.BlockSpec((D, tn), main_col),
            pl.BlockSpec((D, n_small), lambda i, j: (0, 0), pipeline_mode=pl.Buffered(1)),
            pl.BlockSpec((1, KV_LATENT), lambda i, j: (0, 0)),
        ],
        out_specs=[
            pl.BlockSpec((tm, tn), lambda i, j: (i, jnp.minimum(j, n_main - 1))),
            pl.BlockSpec((tm, KV_LATENT), row),
            pl.BlockSpec((tm, IDX_DIM), row),
            pl.BlockSpec((tm, LANES), row),
        ],
        out_shape=[jax.ShapeDtypeStruct((M, n_mainw), BF16), jax.ShapeDtypeStruct((M, KV_LATENT), BF16),
                   jax.ShapeDtypeStruct((M, IDX_DIM), BF16), jax.ShapeDtypeStruct((M, LANES), F32)],
        scratch_shapes=[pltpu.VMEM((tm, D), BF16)],
        compiler_params=_cparams(2),
        name="input_projection",
    )(x, scale.reshape(B, 1, D), shift.reshape(B, 1, D), w_main, w_small, kv_norm_g.reshape(1, KV_LATENT))


def _sortable_key(v):
    bits = pltpu.bitcast(v + 0.0, jnp.int32)
    return bits ^ (lax.shift_right_arithmetic(bits, 31) & jnp.int32(0x7FFFFFFF))


def _dsa_kernel(qidx_ref, kidx_ref, widx_ref, qa_ref, ckv_ref, wuk_ref, wuv_ref, g_ref, o_ref,
                key_ref, wb_ref, q2_ref, qlat_ref, lg_ref, bias_ref, p_ref, acc_ref, m_ref, l_ref,
                alpha_ref, slope_ref, tie_end_ref, obuf_ref, *, tq, tk_score, tk, k_sel, n_heads, n_idx):
    s_len = key_ref.shape[1]
    pos_bits = max(1, (s_len - 1).bit_length())
    i = pl.program_id(1)
    t0 = i * tq
    n_att = (t0 + tq - 1) // tk + 1
    n_score = n_att * (tk // tk_score)

    w = widx_ref[0]
    for h in range(n_idx):
        wb_ref[h] = jnp.broadcast_to(w[:, h:h + 1], (tq, LANES))
        q2_ref[h * tq:(h + 1) * tq, :] = qidx_ref[0, :, h * IDX_DIM:(h + 1) * IDX_DIM]
    t_ids_s = t0 + lax.broadcasted_iota(jnp.int32, (tq, tk_score), 0)
    lane_ids_s = lax.broadcasted_iota(jnp.int32, (tq, tk_score), 1)

    def score_chunk(j, carry):
        s0 = pl.multiple_of(j * tk_score, tk_score)
        kb = kidx_ref[0, pl.ds(s0, tk_score), :]
        lg = lax.dot_general(q2_ref[...], kb, NT_DIMS, preferred_element_type=F32)
        r = jnp.maximum(lg, 0.0).reshape(n_idx, tq, tk_score)
        parts = [jnp.sum(r[:, :, c * LANES:(c + 1) * LANES] * wb_ref[...], axis=0)
                 for c in range(tk_score // LANES)]
        sc = jnp.concatenate(parts, axis=1) * INDEX_SCALE
        sc = jnp.where(s0 + lane_ids_s <= t_ids_s, sc, -jnp.inf)
        key_ref[:, pl.ds(s0, tk_score)] = _sortable_key(sc)
        return carry

    lax.fori_loop(0, n_score, score_chunk, 0)

    lane_ids = lax.broadcasted_iota(jnp.int32, (tq, tk), 1)

    def count_where(preds):
        def chunk(j, cnts):
            s0 = pl.multiple_of(j * tk, tk)
            keys = key_ref[:, pl.ds(s0, tk)]
            s_ids = s0 + lane_ids
            hits = [jnp.where(pred(keys, s_ids), 1.0, 0.0) for pred in preds]
            return tuple(cnt + sum(hit[:, c * LANES:(c + 1) * LANES] for c in range(tk // LANES))
                         for cnt, hit in zip(cnts, hits))
        zero = jnp.zeros((tq, LANES), F32)
        cnts = lax.fori_loop(0, n_att, chunk, (zero,) * len(preds))
        return [jnp.sum(cnt, axis=1, keepdims=True) for cnt in cnts]

    def bisect(it, prefix):
        cand = prefix + lax.shift_left(jnp.int32(1), 31 - it)
        cand_t = _lane_tile(cand, tk)
        (total,) = count_where([lambda keys, s_ids: keys >= cand_t])
        return jnp.where(total >= float(k_sel), cand, prefix)

    thr = lax.fori_loop(0, 32, bisect, jnp.full((tq, LANES), -2 ** 31, jnp.int32))
    thr_t = _lane_tile(thr, tk)

    n_above, n_at_least = count_where([lambda keys, s_ids: keys > thr_t,
                                       lambda keys, s_ids: keys >= thr_t])
    need = float(k_sel) - n_above
    tie_end_ref[...] = jnp.full((tq, LANES), s_len, jnp.int32)

    @pl.when(jnp.max(n_at_least) > float(k_sel))
    def _():
        def grow(it, p):
            cand = p | lax.shift_left(jnp.int32(1), pos_bits - 1 - it)
            cand_t = _lane_tile(cand, tk)
            (below,) = count_where([lambda keys, s_ids: (keys == thr_t) & (s_ids < cand_t)])
            return jnp.where(below < need, cand, p)
        tie_end_ref[...] = lax.fori_loop(0, pos_bits, grow, jnp.zeros((tq, LANES), jnp.int32))

    for h in range(n_heads):
        ql = jnp.dot(qa_ref[0, :, h * HEAD_DIM:(h + 1) * HEAD_DIM], wuk_ref[h], preferred_element_type=F32)
        qlat_ref[h * tq:(h + 1) * tq, :] = (ql * (ATTN_SCALE * LOG2_E)).astype(BF16)
    head_no = lax.broadcasted_iota(jnp.int32, slope_ref.shape, 0).astype(F32)
    slope_ref[...] = jnp.exp2(-8.0 * (head_no + 1.0) / n_heads) * LOG2_E
    m_ref[...] = jnp.full(m_ref.shape, MASKED_LOGIT, F32)
    l_ref[...] = jnp.zeros(l_ref.shape, F32)
    acc_ref[...] = jnp.zeros(acc_ref.shape, F32)
    t_ids = t0 + lax.broadcasted_iota(jnp.int32, (tq, tk), 0)
    col_ids = lax.broadcasted_iota(jnp.int32, (1, tk), 1)

    def attend_chunk(j, carry):
        s0 = pl.multiple_of(j * tk, tk)
        cb = ckv_ref[0, pl.ds(s0, tk), :]
        lg_ref[...] = lax.dot_general(qlat_ref[...], cb, NT_DIMS, preferred_element_type=F32)
        keys = key_ref[:, pl.ds(s0, tk)]
        s_ids = s0 + lane_ids
        sel = (keys > thr_t) | ((keys == thr_t) & (s_ids <= _lane_tile(tie_end_ref[...], tk)))
        sel = sel & (s_ids <= t_ids)
        bias_ref[...] = jnp.where(sel, 0.0, MASKED_LOGIT)
        rel = (s0 - t0 + col_ids).astype(F32)

        def head(h, c2):
            rows = pl.ds(pl.multiple_of(h * tq, tq), tq)
            x = lg_ref[rows, :] + _lane_tile(slope_ref[h], tk) * rel + bias_ref[...]
            m_old = m_ref[rows, :]
            m_new = jnp.maximum(m_old, jnp.max(x, axis=1, keepdims=True))
            alpha = jnp.exp2(m_old - m_new)
            p = jnp.exp2(x - _lane_tile(m_new, tk))
            l_ref[rows, :] = alpha * l_ref[rows, :] + jnp.sum(p, axis=1, keepdims=True)
            m_ref[rows, :] = m_new
            alpha_ref[rows, :] = alpha
            p_ref[rows, :] = p.astype(BF16)
            return c2

        lax.fori_loop(0, n_heads, head, 0, unroll=True)
        lat = acc_ref.shape[1]
        acc_ref[...] = (acc_ref[...] * _lane_tile(alpha_ref[...], lat)
                        + jnp.dot(p_ref[...], cb, preferred_element_type=F32))
        return carry

    lax.fori_loop(0, n_att, attend_chunk, 0)

    ss = jnp.zeros((tq, 1), F32)
    for h in range(n_heads):
        rows = slice(h * tq, (h + 1) * tq)
        inv_l = _lane_tile(1.0 / l_ref[rows, :], acc_ref.shape[1])
        oh = jnp.dot((acc_ref[rows, :] * inv_l).astype(BF16), wuv_ref[h],
                     preferred_element_type=F32)
        obuf_ref[:, h * HEAD_DIM:(h + 1) * HEAD_DIM] = oh
        ss = ss + jnp.sum(oh * oh, axis=1, keepdims=True)
    inv = lax.rsqrt(ss * (1.0 / (n_heads * HEAD_DIM)) + EPS)
    o_ref[0] = (obuf_ref[...] * inv * g_ref[...]).astype(o_ref.dtype)


def dsa_attention(qidx, kidx, widx, qa, ckv, w_uk, w_uv, g, tq=128, tk_score=512, tk=512):
    qidx_arr, qidx_blk = qidx
    qa_arr, qa_blk = qa
    widx_arr, widx_blk = widx
    B, S, lat = ckv.shape
    n_heads = w_uk.shape[0]
    n_idx = IDX_HEADS
    tq, tk = min(tq, S), min(tk, S)
    tk_score = min(tk_score, tk)
    k_sel = min(TOPK_MAX, S // 4)
    width = n_heads * HEAD_DIM
    kernel = functools.partial(_dsa_kernel, tq=tq, tk_score=tk_score, tk=tk, k_sel=k_sel,
                               n_heads=n_heads, n_idx=n_idx)
    once = pl.Buffered(1)
    return pl.pallas_call(
        kernel,
        grid=(B, S // tq),
        in_specs=[
            pl.BlockSpec((1, tq, n_idx * IDX_DIM), lambda b, i: (b, i, qidx_blk)),
            pl.BlockSpec((1, S, IDX_DIM), lambda b, i: (b, 0, 0), pipeline_mode=once),
            pl.BlockSpec((1, tq, LANES), lambda b, i: (b, i, widx_blk)),
            pl.BlockSpec((1, tq, width), lambda b, i: (b, i, qa_blk)),
            pl.BlockSpec((1, S, lat), lambda b, i: (b, 0, 0), pipeline_mode=once),
            pl.BlockSpec((n_heads, HEAD_DIM, lat), lambda b, i: (0, 0, 0), pipeline_mode=once),
            pl.BlockSpec((n_heads, lat, HEAD_DIM), lambda b, i: (0, 0, 0), pipeline_mode=once),
            pl.BlockSpec((1, width), lambda b, i: (0, 0)),
        ],
        out_specs=pl.BlockSpec((1, tq, width), lambda b, i: (b, i, 0)),
        out_shape=jax.ShapeDtypeStruct((B, S, width), BF16),
        scratch_shapes=[
            pltpu.VMEM((tq, S), jnp.int32),
            pltpu.VMEM((n_idx, tq, LANES), F32),
            pltpu.VMEM((n_idx * tq, IDX_DIM), BF16),
            pltpu.VMEM((n_heads * tq, lat), BF16),
            pltpu.VMEM((n_heads * tq, tk), F32),
            pltpu.VMEM((tq, tk), F32),
            pltpu.VMEM((n_heads * tq, tk), BF16),
            pltpu.VMEM((n_heads * tq, lat), F32),
            pltpu.VMEM((n_heads * tq, LANES), F32),
            pltpu.VMEM((n_heads * tq, LANES), F32),
            pltpu.VMEM((n_heads * tq, LANES), F32),
            pltpu.VMEM((n_heads, 1, LANES), F32),
            pltpu.VMEM((tq, LANES), jnp.int32),
            pltpu.VMEM((tq, width), F32),
        ],
        compiler_params=_cparams(2),
        name="dsa_attention",
    )(qidx_arr, kidx, widx_arr, qa_arr, ckv, w_uk, w_uv, g.reshape(1, width))


def _sb_kernel(q_ref, k_ref, v_ref, o_ref, acc_ref, carry_ref, *, tq, hb):
    i = pl.program_id(2)
    t0 = i * tq
    row = lax.broadcasted_iota(jnp.int32, (tq, tq), 0)
    col = lax.broadcasted_iota(jnp.int32, (tq, tq), 1)
    suffix_ones = jnp.where(row > col, 1.0, 0.0).astype(BF16)
    acc_ref[...] = jnp.zeros(acc_ref.shape, F32)
    carry_ref[...] = jnp.zeros(carry_ref.shape, F32)

    def keep_going(state):
        j, carry_min = state
        return (j >= 0) & (carry_min <= F32_EXP_UNDERFLOW)

    def block(state):
        j, _ = state
        s0 = pl.multiple_of(j * tq, tq)
        strict = (s0 + col) < (t0 + row)
        carry_min = None
        for h in range(hb):
            cols = slice(h * HEAD_DIM, (h + 1) * HEAD_DIM)
            kb = k_ref[0, pl.ds(s0, tq), cols]
            vb = v_ref[0, pl.ds(s0, tq), cols]
            z = lax.dot_general(q_ref[0, :, cols], kb, NT_DIMS, preferred_element_type=F32) * ATTN_SCALE
            sp = jnp.maximum(z, 0.0) + jnp.log(1.0 + jnp.exp(-jnp.abs(z)))
            u = jnp.where(strict, sp, 0.0)
            u_hi, u_lo = _split_bf16(u)
            within = (jnp.dot(u_hi, suffix_ones, preferred_element_type=F32)
                      + jnp.dot(u_lo, suffix_ones, preferred_element_type=F32))
            carry = carry_ref[:, cols]
            a = jnp.where(strict, jnp.exp(z - sp - (_lane_tile(carry, tq) + within)), 0.0)
            acc_ref[:, cols] += jnp.dot(a.astype(BF16), vb, preferred_element_type=F32)
            carry = carry + jnp.sum(u, axis=1, keepdims=True)
            carry_ref[:, cols] = carry
            head_min = jnp.min(carry)
            carry_min = head_min if carry_min is None else jnp.minimum(carry_min, head_min)
        return j - 1, carry_min

    lax.while_loop(keep_going, block, (i, jnp.float32(0.0)))
    o_ref[0] = acc_ref[...]


SB_HEADS_PER_STEP = 8


def stick_breaking_attention(q, k, v, n_heads, tq=256, hb=SB_HEADS_PER_STEP):
    (q_arr, q_blk), (k_arr, k_blk), (v_arr, v_blk) = q, k, v
    B, S, _ = q_arr.shape
    tq = min(tq, S)
    wb = hb * HEAD_DIM
    once = pl.Buffered(1)
    return pl.pallas_call(
        functools.partial(_sb_kernel, tq=tq, hb=hb),
        grid=(B, n_heads // hb, S // tq),
        in_specs=[pl.BlockSpec((1, tq, wb), lambda b, h, i: (b, i, q_blk + h)),
                  pl.BlockSpec((1, S, wb), lambda b, h, i: (b, 0, k_blk + h), pipeline_mode=once),
                  pl.BlockSpec((1, S, wb), lambda b, h, i: (b, 0, v_blk + h), pipeline_mode=once)],
        out_specs=pl.BlockSpec((1, tq, wb), lambda b, h, i: (b, i, h)),
        out_shape=jax.ShapeDtypeStruct((B, S, n_heads * HEAD_DIM), F32),
        scratch_shapes=[pltpu.VMEM((tq, wb), F32), pltpu.VMEM((tq, wb), F32)],
        compiler_params=_cparams(3),
        name="stick_breaking_attention",
    )(q_arr, k_arr, v_arr)


def _layer_norm(v, g, b):
    mu = jnp.mean(v, axis=-1, keepdims=True)
    d = v - mu
    var = jnp.mean(d * d, axis=-1, keepdims=True)
    return d * lax.rsqrt(var + EPS) * g + b


def _outproj_ln_kernel(od_ref, os_ref, gsb_ref, w_ref, x_ref, gate_ref, g_ref, b_ref, sc_ref, sh_ref,
                       wr_ref, x1_ref, h2_ref, lg_ref, a_ref, *, tn):
    j = pl.program_id(1)
    wd = od_ref.shape[1]

    @pl.when(j == 0)
    def _():
        a_ref[:, :wd] = od_ref[...]
        o = os_ref[...]
        ms = jnp.mean(o * o, axis=-1, keepdims=True)
        a_ref[:, wd:] = (o * lax.rsqrt(ms + EPS) * gsb_ref[...]).astype(BF16)

    cols = pl.ds(pl.multiple_of(j * tn, tn), tn)
    x1_ref[:, cols] = jnp.dot(a_ref[...], w_ref[:, cols], preferred_element_type=F32)

    @pl.when(j == pl.num_programs(1) - 1)
    def _():
        y = _layer_norm(DEEPNORM_ALPHA * x_ref[...] + gate_ref[0] * x1_ref[...], g_ref[...], b_ref[...])
        x1_ref[...] = y
        h2 = (y * (1.0 + sc_ref[0]) + sh_ref[0]).astype(BF16)
        lg_ref[...] = jnp.dot(h2, wr_ref[...], preferred_element_type=F32)
        h2_ref[...] = pltpu.einshape("t(sl)->tsl", h2, s=h2_ref.shape[1])


def outproj_ln(o_dsa, o_sb, g_sb, w_out, x, gate, ln_g, ln_b, scale, shift, w_router, seq_len,
               tm=256, tn=512):
    M, D = x.shape
    wd, ws = o_dsa.shape[1], o_sb.shape[1]
    B = gate.shape[0]
    tm, tn = min(tm, seq_len), min(tn, D)
    n_exp = w_router.shape[1]
    wr = jnp.zeros((D, LANES), BF16).at[:, :n_exp].set(w_router.astype(BF16))

    def per_batch(i, j):
        return (i * tm // seq_len, 0, 0)

    row = lambda i, j: (i, 0)
    shared = lambda i, j: (0, 0)
    once = pl.Buffered(1)
    x1, h2_slabs, logits = pl.pallas_call(
        functools.partial(_outproj_ln_kernel, tn=tn),
        grid=(M // tm, D // tn),
        in_specs=[
            pl.BlockSpec((tm, wd), row),
            pl.BlockSpec((tm, ws), row, pipeline_mode=once),
            pl.BlockSpec((1, ws), shared),
            pl.BlockSpec((wd + ws, D), shared, pipeline_mode=once),
            pl.BlockSpec((tm, D), row, pipeline_mode=once),
            pl.BlockSpec((1, 1, D), per_batch),
            pl.BlockSpec((1, D), shared),
            pl.BlockSpec((1, D), shared),
            pl.BlockSpec((1, 1, D), per_batch),
            pl.BlockSpec((1, 1, D), per_batch),
            pl.BlockSpec((D, LANES), shared),
        ],
        out_specs=[pl.BlockSpec((tm, D), row),
                   pl.BlockSpec((tm, D // LANES, LANES), lambda i, j: (i, 0, 0)),
                   pl.BlockSpec((tm, LANES), row)],
        out_shape=[jax.ShapeDtypeStruct((M, D), F32),
                   jax.ShapeDtypeStruct((M, D // LANES, LANES), BF16),
                   jax.ShapeDtypeStruct((M, LANES), F32)],
        scratch_shapes=[pltpu.VMEM((tm, wd + ws), BF16)],
        compiler_params=pltpu.CompilerParams(
            dimension_semantics=("arbitrary", "arbitrary"), vmem_limit_bytes=VMEM_LIMIT_WIDE_ROWS_BYTES),
        name="outproj_ln",
    )(o_dsa, o_sb, g_sb.reshape(1, ws), w_out, x, gate.reshape(B, 1, D), ln_g.reshape(1, D),
      ln_b.reshape(1, D), scale.reshape(B, 1, D), shift.reshape(B, 1, D), wr)
    return x1, h2_slabs, logits[:, :n_exp]


def _scatter_rows_kernel(pos_ref, pad_end_ref, cnt_ref, h_ref, xg_ref, zero_ref, sem, *, tm, tt, top_k,
                         n_exp):
    i = pl.program_id(0)

    def fill_copy(e):
        start = pl.multiple_of(pad_end_ref[e] - tm, tm)
        return pltpu.make_async_copy(zero_ref, xg_ref.at[pl.ds(start, tm)], sem.at[0])

    @pl.when(i == 0)
    def _():
        zero_ref[...] = jnp.zeros(zero_ref.shape, zero_ref.dtype)

        def start_fill(e, c):
            @pl.when(cnt_ref[e] % tm != 0)
            def _():
                fill_copy(e).start()
            return c

        def wait_fill(e, c):
            @pl.when(cnt_ref[e] % tm != 0)
            def _():
                fill_copy(e).wait()
            return c

        def tail_copy(r):
            return pltpu.make_async_copy(
                zero_ref, xg_ref.at[pl.ds(pl.multiple_of(r * tm, tm), tm)], sem.at[0])

        def start_tail(r, c):
            tail_copy(r).start()
            return c

        def wait_tail(r, c):
            tail_copy(r).wait()
            return c

        n_used = pad_end_ref[n_exp - 1] // tm
        n_blocks = xg_ref.shape[0] // tm
        lax.fori_loop(0, n_exp, start_fill, 0)
        lax.fori_loop(n_used, n_blocks, start_tail, 0)
        lax.fori_loop(0, n_exp, wait_fill, 0)
        lax.fori_loop(n_used, n_blocks, wait_tail, 0)

    base = i * tt

    def row_copy(t, k):
        return pltpu.make_async_copy(h_ref.at[t], xg_ref.at[pos_ref[(base + t) * top_k + k]], sem.at[1])

    def start_rows(t, c):
        for k in range(top_k):
            row_copy(t, k).start(priority=k % 2)
        return c

    def wait_rows(t, c):
        for k in range(top_k):
            row_copy(t, k).wait()
        return c

    lax.fori_loop(0, tt, start_rows, 0, unroll=4)
    lax.fori_loop(0, tt, wait_rows, 0, unroll=4)


def scatter_rows(pos, pad_end, counts, h_slabs, n_rows, tm, tt=256):
    n_tok, P, _ = h_slabs.shape
    dtype = h_slabs.dtype
    tt = min(tt, n_tok)
    kernel = functools.partial(_scatter_rows_kernel, tm=tm, tt=tt, top_k=TOP_K_EXPERTS, n_exp=N_EXPERTS)
    return pl.pallas_call(
        kernel,
        grid_spec=pltpu.PrefetchScalarGridSpec(
            num_scalar_prefetch=3,
            grid=(n_tok // tt,),
            in_specs=[pl.BlockSpec((tt, P, LANES), lambda i, p, e, c: (i, 0, 0))],
            out_specs=pl.BlockSpec(memory_space=pl.ANY),
            scratch_shapes=[pltpu.VMEM((tm, P, LANES), dtype), pltpu.SemaphoreType.DMA((2,))],
        ),
        out_shape=jax.ShapeDtypeStruct((n_rows, P, LANES), dtype),
        compiler_params=_cparams(1),
        name="scatter_rows",
    )(pos, pad_end, counts, h_slabs)


def _expert_changed(blk_e_ref, r):
    return (r == 0) | (blk_e_ref[r] != blk_e_ref[jnp.maximum(r - 1, 0)])


def _moe_up_kernel(blk_e_ref, n_used_ref, x_ref, wg_ref, wl_ref, bg_ref, bl_ref, o_ref, wg_bf, wl_bf):
    r = pl.program_id(1)
    used = r < n_used_ref[0]

    @pl.when(used & _expert_changed(blk_e_ref, r))
    def _():
        wg_bf[...] = wg_ref[0].astype(BF16)
        wl_bf[...] = wl_ref[0].astype(BF16)

    @pl.when(used)
    def _():
        x = pltpu.einshape("tsl->t(sl)", x_ref[...])
        glu = jnp.dot(x, wg_bf[...], preferred_element_type=F32) + bg_ref[0]
        lin = jnp.dot(x, wl_bf[...], preferred_element_type=F32) + bl_ref[0]
        glu = jnp.minimum(glu, SWIGLU_LIMIT)
        lin = jnp.clip(lin, -SWIGLU_LIMIT, SWIGLU_LIMIT)
        act = glu * (1.0 / (1.0 + jnp.exp(-SWIGLU_ALPHA * glu))) * (lin + 1.0)
        o_ref[...] = act.astype(o_ref.dtype)

    @pl.when(jnp.logical_not(used))
    def _():
        o_ref[...] = jnp.zeros(o_ref.shape, o_ref.dtype)


def _moe_down_kernel(blk_e_ref, n_used_ref, a_ref, wd_ref, bd_ref, o_ref, wd_bf):
    r = pl.program_id(1)
    used = r < n_used_ref[0]

    @pl.when(used & _expert_changed(blk_e_ref, r))
    def _():
        wd_bf[...] = wd_ref[0].astype(BF16)

    @pl.when(used)
    def _():
        out = jnp.dot(a_ref[...], wd_bf[...], preferred_element_type=F32) + bd_ref[0]
        o_ref[...] = pltpu.einshape("t(sl)->tsl", out, s=o_ref.shape[1])

    @pl.when(jnp.logical_not(used))
    def _():
        o_ref[...] = jnp.zeros(o_ref.shape, o_ref.dtype)


def moe_experts(xg, blk_e, n_used, w_up, b_up, w_down, b_down, tm, tn_up=512, tn_down=2048):
    R, P, _ = xg.shape
    D = P * LANES
    n_exp, d_exp = w_down.shape[0], w_down.shape[1]
    tn_up, tn_down = min(tn_up, d_exp), min(tn_down, D)
    n_up = d_exp // tn_up

    def row(r, n):
        return jnp.minimum(r, n[0] - 1)

    def expert(r, e, n):
        return e[row(r, n)]

    b_up3 = b_up.reshape(n_exp, 1, 2 * d_exp)
    act = pl.pallas_call(
        _moe_up_kernel,
        grid_spec=pltpu.PrefetchScalarGridSpec(
            num_scalar_prefetch=2,
            grid=(n_up, R // tm),
            in_specs=[
                pl.BlockSpec((tm, P, LANES), lambda c, r, e, n: (row(r, n), 0, 0)),
                pl.BlockSpec((1, D, tn_up), lambda c, r, e, n: (expert(r, e, n), 0, c)),
                pl.BlockSpec((1, D, tn_up), lambda c, r, e, n: (expert(r, e, n), 0, n_up + c)),
                pl.BlockSpec((1, 1, tn_up), lambda c, r, e, n: (expert(r, e, n), 0, c)),
                pl.BlockSpec((1, 1, tn_up), lambda c, r, e, n: (expert(r, e, n), 0, n_up + c)),
            ],
            out_specs=pl.BlockSpec((tm, tn_up), lambda c, r, e, n: (r, c)),
            scratch_shapes=[pltpu.VMEM((D, tn_up), BF16), pltpu.VMEM((D, tn_up), BF16)],
        ),
        out_shape=jax.ShapeDtypeStruct((R, d_exp), BF16),
        compiler_params=_cparams(2),
        name="moe_up",
    )(blk_e, n_used, xg, w_up, w_up, b_up3, b_up3)
    return pl.pallas_call(
        _moe_down_kernel,
        grid_spec=pltpu.PrefetchScalarGridSpec(
            num_scalar_prefetch=2,
            grid=(D // tn_down, R // tm),
            in_specs=[
                pl.BlockSpec((tm, d_exp), lambda c, r, e, n: (row(r, n), 0)),
                pl.BlockSpec((1, d_exp, tn_down), lambda c, r, e, n: (expert(r, e, n), 0, c)),
                pl.BlockSpec((1, 1, tn_down), lambda c, r, e, n: (expert(r, e, n), 0, c)),
            ],
            out_specs=pl.BlockSpec((tm, tn_down // LANES, LANES), lambda c, r, e, n: (r, c, 0)),
            scratch_shapes=[pltpu.VMEM((d_exp, tn_down), BF16)],
        ),
        out_shape=jax.ShapeDtypeStruct((R, D // LANES, LANES), F32),
        compiler_params=_cparams(2),
        name="moe_down",
    )(blk_e, n_used, act, w_down, b_down.reshape(n_exp, 1, D))


def _combine_ln_kernel(pos_ref, gates_ref, rows_ref, x_ref, gate_ref, g_ref, b_ref, o_ref,
                       buf0, buf1, ysl_ref, y_ref, sem, *, tt, top_k):
    i = pl.program_id(0)
    n_blocks = pl.num_programs(0)

    def row_copy(blk, buf, slot, t, k):
        src = rows_ref.at[pos_ref[(blk * tt + t) * top_k + k]]
        return pltpu.make_async_copy(src, buf.at[k * tt + t], sem.at[slot])

    def start_block(blk, buf, slot):
        def body(t, c):
            for k in range(top_k):
                row_copy(blk, buf, slot, t, k).start(priority=k % 2)
            return c
        lax.fori_loop(0, tt, body, 0, unroll=4)

    def wait_block(blk, buf, slot):
        def body(t, c):
            for k in range(top_k):
                row_copy(blk, buf, slot, t, k).wait()
            return c
        lax.fori_loop(0, tt, body, 0, unroll=4)

    def combine(buf):
        def per_token(t, c):
            base = (i * tt + t) * top_k
            acc = gates_ref[base] * buf[t]
            for k in range(1, top_k):
                acc = acc + gates_ref[base + k] * buf[k * tt + t]
            ysl_ref[t] = acc
            return c
        lax.fori_loop(0, tt, per_token, 0, unroll=4)
        y_ref[...] = pltpu.einshape("tsl->t(sl)", ysl_ref[...])

    even = i % 2 == 0

    @pl.when(i == 0)
    def _():
        start_block(0, buf0, 0)

    @pl.when((i + 1 < n_blocks) & even)
    def _():
        start_block(i + 1, buf1, 1)

    @pl.when((i + 1 < n_blocks) & jnp.logical_not(even))
    def _():
        start_block(i + 1, buf0, 0)

    @pl.when(even)
    def _():
        wait_block(i, buf0, 0)
        combine(buf0)

    @pl.when(jnp.logical_not(even))
    def _():
        wait_block(i, buf1, 1)
        combine(buf1)

    o_ref[...] = _layer_norm(DEEPNORM_ALPHA * x_ref[...] + gate_ref[0] * y_ref[...], g_ref[...], b_ref[...])


def combine_ln(pos, gates, out_rows, x, gate, ln_g, ln_b, seq_len, tt=128):
    n_tok, D = x.shape
    B = gate.shape[0]
    tt = min(tt, seq_len)
    P = out_rows.shape[1]
    kernel = functools.partial(_combine_ln_kernel, tt=tt, top_k=TOP_K_EXPERTS)
    return pl.pallas_call(
        kernel,
        grid_spec=pltpu.PrefetchScalarGridSpec(
            num_scalar_prefetch=2,
            grid=(n_tok // tt,),
            in_specs=[
                pl.BlockSpec(memory_space=pl.ANY),
                pl.BlockSpec((tt, D), lambda i, p, q: (i, 0)),
                pl.BlockSpec((1, 1, D), lambda i, p, q: (i * tt // seq_len, 0, 0)),
                pl.BlockSpec((1, D), lambda i, p, q: (0, 0)),
                pl.BlockSpec((1, D), lambda i, p, q: (0, 0)),
            ],
            out_specs=pl.BlockSpec((tt, D), lambda i, p, q: (i, 0)),
            scratch_shapes=[pltpu.VMEM((TOP_K_EXPERTS * tt, P, LANES), F32),
                            pltpu.VMEM((TOP_K_EXPERTS * tt, P, LANES), F32),
                            pltpu.VMEM((tt, P, LANES), F32),
                            pltpu.VMEM((tt, D), F32),
                            pltpu.SemaphoreType.DMA((2,))],
        ),
        out_shape=jax.ShapeDtypeStruct((n_tok, D), F32),
        compiler_params=_cparams(1),
        name="combine_ln",
    )(pos, gates, out_rows, x, gate.reshape(B, 1, D), ln_g.reshape(1, D), ln_b.reshape(1, D))


def route(logits, tm):
    n_tok = logits.shape[0]
    top_logit, top_e = lax.top_k(logits, TOP_K_EXPERTS)
    gates = jax.nn.softmax(top_logit, axis=-1)
    n_asg = n_tok * TOP_K_EXPERTS
    e_flat = top_e.reshape(n_asg)
    onehot = (e_flat[:, None] == jnp.arange(N_EXPERTS, dtype=e_flat.dtype)[None, :]).astype(jnp.int32)
    assert n_asg % LANES == 0
    blocks = onehot.reshape(n_asg // LANES, LANES, N_EXPERTS).astype(F32)
    lower = jnp.tril(jnp.ones((LANES, LANES), F32))
    within = jnp.einsum('ij,bjk->bik', lower, blocks)
    block_tot = within[:, -1, :]
    before = jnp.cumsum(block_tot, axis=0) - block_tot
    running = (within + before[:, None, :]).reshape(n_asg, N_EXPERTS).astype(jnp.int32)
    rank = jnp.take_along_axis(running, e_flat[:, None], axis=1)[:, 0] - 1
    counts = running[-1]
    padded = (counts + tm - 1) // tm * tm
    pad_end = jnp.cumsum(padded)
    pad_start = pad_end - padded
    pos = (pad_start[e_flat] + rank).astype(jnp.int32)
    n_blocks = -(-n_asg // tm) + N_EXPERTS
    first_row = jnp.arange(n_blocks, dtype=pad_end.dtype) * tm
    blk_e = jnp.minimum(jnp.sum(pad_end[None, :] <= first_row[:, None], axis=1),
                        N_EXPERTS - 1).astype(jnp.int32)
    n_used = (pad_end[-1] // tm).astype(jnp.int32).reshape(1)
    return (gates, pos, blk_e, n_used, pad_end.astype(jnp.int32), counts.astype(jnp.int32),
            n_blocks * tm)


def moe_block(h2_slabs, logits, x1, gate_f, ln_g, ln_b, w_up, b_up, w_down, b_down, seq_len, tm=512):
    gates, pos, blk_e, n_used, pad_end, counts, n_rows = route(logits, tm)
    xg = scatter_rows(pos, pad_end, counts, h2_slabs, n_rows, tm)
    out_rows = moe_experts(xg, blk_e, n_used, w_up, b_up, w_down, b_down, tm)
    return combine_ln(pos, gates.reshape(-1), out_rows, x1, gate_f, ln_g, ln_b, seq_len)


def _layer(x, c, w_ada, b_ada, w_in, kv_norm_g, w_uk, w_uv, grp_norm_dsa, grp_norm_sb, w_out,
           ln1_g, ln1_b, w_router, b_router, w_up, b_up, w_down, b_down, ln2_g, ln2_b):
    B, S, D = x.shape
    n_tok = B * S
    n_dsa = w_uk.shape[0]
    w_qa = n_dsa * HEAD_DIM
    w_qidx = IDX_HEADS * IDX_DIM
    w_sb = grp_norm_sb.shape[0]
    n_sb = w_sb // HEAD_DIM

    mod = ada_modulation(c, w_ada, b_ada)
    shift_a, scale_a, gate_a, shift_f, scale_f, gate_f = jnp.split(mod, 6, axis=-1)

    o1 = w_qa
    o2 = o1 + KV_LATENT
    o3 = o2 + w_qidx
    o4 = o3 + IDX_DIM
    o5 = o4 + IDX_HEADS
    w_main = jnp.concatenate([w_in[:, o2:o3], w_in[:, :o1], w_in[:, o5:]], axis=1).astype(BF16)
    n_small = KV_LATENT + IDX_DIM + IDX_HEADS
    n_small_pad = KV_LATENT + IDX_DIM + LANES
    w_small = jnp.concatenate(
        [w_in[:, o1:o2], w_in[:, o3:o5], jnp.zeros((D, n_small_pad - n_small), F32)], axis=1).astype(BF16)

    proj, ckv, kidx, widx = input_projection(x.reshape(n_tok, D), scale_a, shift_a, w_main, w_small,
                                             kv_norm_g, S)
    proj = proj.reshape(B, S, w_main.shape[1])

    sb_w = SB_HEADS_PER_STEP * HEAD_DIM
    o_dsa = dsa_attention(
        qidx=(proj, 0), kidx=kidx.reshape(B, S, IDX_DIM), widx=(widx.reshape(B, S, LANES), 0),
        qa=(proj, w_qidx // w_qa), ckv=ckv.reshape(B, S, KV_LATENT),
        w_uk=w_uk.astype(BF16), w_uv=w_uv.astype(BF16), g=grp_norm_dsa)
    sb0 = w_qidx + w_qa
    o_sb = stick_breaking_attention((proj, sb0 // sb_w), (proj, (sb0 + w_sb) // sb_w),
                                    (proj, (sb0 + 2 * w_sb) // sb_w), n_sb)

    x1, h2_slabs, logits = outproj_ln(
        o_dsa.reshape(n_tok, w_qa), o_sb.reshape(n_tok, w_sb), grp_norm_sb, w_out.astype(BF16),
        x.reshape(n_tok, D), gate_a, ln1_g, ln1_b, scale_f, shift_f, w_router, S)
    out = moe_block(h2_slabs, logits + b_router.astype(F32), x1, gate_f, ln2_g, ln2_b,
                    w_up, b_up, w_down, b_down, S)
    return out.reshape(B, S, D)


def kernel(x, c, w_ada, b_ada, w_in, kv_norm_g, w_uk, w_uv, grp_norm_dsa, grp_norm_sb, w_out, ln1_g, ln1_b, w_router, b_router, w_up, b_up, w_down, b_down, ln2_g, ln2_b):
    return _layer(x, c, w_ada[0], b_ada[0], w_in[0], kv_norm_g[0], w_uk[0], w_uv[0],
                  grp_norm_dsa[0], grp_norm_sb[0], w_out[0], ln1_g[0], ln1_b[0], w_router[0],
                  b_router[0], w_up[0], b_up[0], w_down[0], b_down[0], ln2_g[0], ln2_b[0])
```
